```python
import math
import jax, jax.numpy as jnp
from jax import lax
import numpy as np

D_MODEL = 1024
BATCH = 8
SEQ = 2048
DEPTH = 4
DEC_BATCH = 128
DEC_SEQ = 1
PAST_LEN = 8192
PAGE_SIZE = 128

N_MIXERS = 3
N_A = len(range(0, DEPTH, N_MIXERS))
N_B = len(range(1, DEPTH, N_MIXERS))
N_C = len(range(2, DEPTH, N_MIXERS))
A_WINDOWS = (128, 512, 2048)
A_DILATIONS = (1, 4, 16)
A_GROUPS = 3
A_HEADS = 8
A_HD = 64
B_WINDOW = 128
B_HEADS = 16
B_KV_HEADS = 2
B_GROUP = B_HEADS // B_KV_HEADS
B_HD = 64
C_HEADS = 8
C_KV_HEADS = 4
C_GROUP = C_HEADS // C_KV_HEADS
C_HD = 64
C_VD = 2 * C_HD
NUM_BUCKETS = 32
MAX_DISTANCE = 2048
BIAS_HEADS = 8
N_GROUPS = 4
EXPERTS_PER_GROUP = 8
N_EXPERTS = N_GROUPS * EXPERTS_PER_GROUP
D_EXPERT = 512
TOP_K = 2
MOE_CHUNK = 1024
QBLOCK = 128
NORM_EPS = 1e-6
NEG = -1e30

kernel_name = 'hybrid_dilated_swa_diffattn_hmoe_step'


def rmsnorm(x, g):
    xf = x.astype(jnp.float32)
    y = xf * lax.rsqrt(jnp.mean(xf * xf, axis=-1, keepdims=True) + NORM_EPS)
    return (y * g.astype(jnp.float32)).astype(x.dtype)


def t5_bucket(dist):
    max_exact = NUM_BUCKETS // 2
    d = jnp.maximum(dist, 0)
    ratio = jnp.log(jnp.maximum(d, 1).astype(jnp.float32) / max_exact) / math.log(MAX_DISTANCE / max_exact)
    large = jnp.minimum(max_exact + (ratio * (NUM_BUCKETS - max_exact)).astype(jnp.int32), NUM_BUCKETS - 1)
    return jnp.where(d < max_exact, d, large)


def rel_bias(table, dist):
    return table.astype(jnp.float32)[t5_bucket(dist)]


def ada(c, w, b):
    m = jax.nn.silu(c) @ w + b
    return jnp.split(m[:, None, :], 6, axis=-1)


def modulate(h, shift, scale):
    return h * (1 + scale) + shift


def _a_project(h, w_in):
    B, T, _ = h.shape
    qkv = (h @ w_in).reshape(B, T, 3, A_GROUPS, A_HEADS, A_HD)
    return qkv[:, :, 0], qkv[:, :, 1], qkv[:, :, 2]


def _a_bias(table):
    return [rel_bias(table, r * jnp.arange(w // r + 1, dtype=jnp.int32)).T
            for w, r in zip(A_WINDOWS, A_DILATIONS)]


def _a_group(q, k, v, q_local, dil, bias):
    n_keys = bias.shape[-1]
    idx = q_local[:, None] - dil * jnp.arange(n_keys, dtype=jnp.int32)[None, :]
    valid = idx >= 0
    idx = jnp.maximum(idx, 0)
    kg = jnp.take(k, idx, axis=1)
    vg = jnp.take(v, idx, axis=1)
    s = jnp.einsum('bqhd,bqmhd->bhqm', q, kg).astype(jnp.float32) * (A_HD ** -0.5)
    s = jnp.where(valid[None, None], s + bias[None, :, None, :], NEG)
    lse = jax.nn.logsumexp(s, axis=-1)
    p = jnp.exp(s - lse[..., None]).astype(v.dtype)
    return jnp.einsum('bhqm,bqmhd->bqhd', p, vg), lse


def _a_core(qs, ks, vs, q_locals, biases):
    outs, lses = [], []
    for g in range(A_GROUPS):
        o, l = _a_group(qs[g], ks[g], vs[g], q_locals[g], A_DILATIONS[g], biases[g])
        outs.append(o)
        lses.append(l)
    w = jax.nn.softmax(jnp.stack(lses), axis=0).transpose(0, 1, 3, 2)[..., None]
    outs = jnp.stack(outs)
    return jnp.sum(w.astype(outs.dtype) * outs, axis=0)


def mixer_a_prompt(h, w_in, w_out, table):
    B, S, _ = h.shape
    q, k, v = _a_project(h, w_in)
    biases = _a_bias(table)
    ks = [k[:, :, g] for g in range(A_GROUPS)]
    vs = [v[:, :, g] for g in range(A_GROUPS)]
    nb = S // QBLOCK
    qb = q.reshape(B, nb, QBLOCK, A_GROUPS, A_HEADS, A_HD).swapaxes(0, 1)

    def block(args):
        qi, bi = args
        q_local = bi * QBLOCK + jnp.arange(QBLOCK, dtype=jnp.int32)
        return _a_core([qi[:, :, g] for g in range(A_GROUPS)], ks, vs, [q_local] * A_GROUPS, biases)

    o = lax.map(block, (qb, jnp.arange(nb, dtype=jnp.int32)))
    o = o.swapaxes(0, 1).reshape(B, S, A_HEADS * A_HD)
    new = []
    for g, w in enumerate(A_WINDOWS):
        L = min(w, S)
        new.append(jnp.stack([ks[g][:, S - L:], vs[g][:, S - L:]], axis=2))
    return o @ w_out, new


def mixer_a_sample(h, w_in, w_out, table, bufs):
    B, T, _ = h.shape
    q, k, v = _a_project(h, w_in)
    biases = _a_bias(table)
    ks, vs, qls, new = [], [], [], []
    for g, buf in enumerate(bufs):
        L = buf.shape[1]
        kc = jnp.concatenate([buf[:, :, 0], k[:, :, g]], axis=1)
        vc = jnp.concatenate([buf[:, :, 1], v[:, :, g]], axis=1)
        ks.append(kc)
        vs.append(vc)
        qls.append(L + jnp.arange(T, dtype=jnp.int32))
        new.append(jnp.stack([kc[:, -L:], vc[:, -L:]], axis=2))
    o = _a_core([q[:, :, g] for g in range(A_GROUPS)], ks, vs, qls, biases)
    return o.reshape(B, T, A_HEADS * A_HD) @ w_out, new


def _b_project(h, w_in):
    B, T, _ = h.shape
    qkv = h @ w_in
    nq, nk = B_HEADS * B_HD, B_KV_HEADS * B_HD
    q = qkv[..., :nq].reshape(B, T, B_KV_HEADS, B_GROUP, B_HD)
    k = qkv[..., nq:nq + nk].reshape(B, T, B_KV_HEADS, B_HD)
    v = qkv[..., nq + nk:].reshape(B, T, B_KV_HEADS, B_HD)
    return q, k, v


def _b_core(q, k, v, dist, valid, sinks, table):
    tq, tk = dist.shape
    cols = jnp.arange(B_HEADS) // (B_HEADS // BIAS_HEADS)
    bias = rel_bias(table, dist)[..., cols].transpose(2, 0, 1).reshape(B_KV_HEADS, B_GROUP, tq, tk)
    s = jnp.einsum('...qkgd,...tkd->...kgqt', q, k).astype(jnp.float32) * (B_HD ** -0.5) + bias
    s = jnp.where(valid[..., None, None, :, :], s, NEG)
    sink = sinks.astype(jnp.float32).reshape(B_KV_HEADS, B_GROUP, 1, 1)
    m = jnp.maximum(jnp.max(s, axis=-1, keepdims=True), sink)
    p = jnp.exp(s - m)
    p = p / (jnp.sum(p, axis=-1, keepdims=True) + jnp.exp(sink - m))
    return jnp.einsum('...kgqt,...tkd->...qkgd', p.astype(v.dtype), v)


def mixer_b_prompt(h, w_in, sinks, w_out, table):
    B, S, _ = h.shape
    q, k, v = _b_project(h, w_in)
    W = B_WINDOW
    nb = S // W

    def band(x):
        xb = x.reshape(B, nb, W, B_KV_HEADS, B_HD)
        prev = jnp.concatenate([jnp.zeros_like(xb[:, :1]), xb[:, :-1]], axis=1)
        return jnp.concatenate([prev, xb], axis=2)

    i = jnp.arange(W, dtype=jnp.int32)[:, None]
    j = jnp.arange(2 * W, dtype=jnp.int32)[None, :]
    dist = W + i - j
    valid = ((dist >= 0) & (dist <= B_WINDOW))[None] & ((jnp.arange(nb)[:, None, None] > 0) | (j[None] >= W))
    o = _b_core(q.reshape(B, nb, W, B_KV_HEADS, B_GROUP, B_HD), band(k), band(v), dist, valid, sinks, table)
    o = o.reshape(B, S, B_HEADS * B_HD)
    L = min(B_WINDOW, S)
    return o @ w_out, jnp.stack([k[:, S - L:], v[:, S - L:]], axis=2)


def mixer_b_sample(h, w_in, sinks, w_out, table, buf):
    B, T, _ = h.shape
    q, k, v = _b_project(h, w_in)
    L = buf.shape[1]
    kc = jnp.concatenate([buf[:, :, 0], k], axis=1)
    vc = jnp.concatenate([buf[:, :, 1], v], axis=1)
    dist = (L + jnp.arange(T, dtype=jnp.int32))[:, None] - jnp.arange(L + T, dtype=jnp.int32)[None, :]
    valid = (dist >= 0) & (dist <= B_WINDOW)
    o = _b_core(q, kc, vc, dist, valid, sinks, table).reshape(B, T, B_HEADS * B_HD)
    return o @ w_out, jnp.stack([kc[:, -L:], vc[:, -L:]], axis=2)


def _c_project(h, w_in):
    B, T, _ = h.shape
    qkv = h @ w_in
    nq, nk = C_HEADS * 2 * C_HD, C_KV_HEADS * C_VD
    q = qkv[..., :nq].reshape(B, T, C_KV_HEADS, C_GROUP, 2, C_HD)
    k = qkv[..., nq:nq + nk].reshape(B, T, C_KV_HEADS, C_VD)
    v = qkv[..., nq + nk:].reshape(B, T, C_KV_HEADS, C_VD)
    return q, k, v


def _c_stats(q, k, v, dist, valid, table):
    tq, tk = dist.shape
    kk = k.reshape(k.shape[:3] + (2, C_HD))
    bias = rel_bias(table, dist).transpose(2, 0, 1).reshape(C_KV_HEADS, C_GROUP, 1, tq, tk)
    s = jnp.einsum('bqkgmd,btkmd->bkgmqt', q, kk).astype(jnp.float32) * (C_HD ** -0.5) + bias
    s = jnp.where(valid, s, NEG)
    mx = jnp.max(s, axis=-1)
    p = jnp.exp(s - mx[..., None])
    acc = jnp.einsum('bkgmqt,btkd->bkgmqd', p.astype(v.dtype), v).astype(jnp.float32)
    return mx, jnp.sum(p, axis=-1), acc


def _c_merge(a, b):
    m = jnp.maximum(a[0], b[0])
    ea, eb = jnp.exp(a[0] - m), jnp.exp(b[0] - m)
    return m, a[1] * ea + b[1] * eb, a[2] * ea[..., None] + b[2] * eb[..., None]


def _diff_lambda(lam_p, lam_init):
    lp = lam_p.astype(jnp.float32)
    return jnp.exp(jnp.sum(lp[0] * lp[1])) - jnp.exp(jnp.sum(lp[2] * lp[3])) + lam_init


def _c_combine(stats, lam, subln, lam_init):
    _, l, acc = stats
    o = acc / l[..., None]
    o = o[:, :, :, 0] - lam * o[:, :, :, 1]
    B, _, _, T, _ = o.shape
    o = o.transpose(0, 3, 1, 2, 4).reshape(B, T, C_HEADS, C_VD)
    return rmsnorm(o, subln) * (1.0 - lam_init)


def mixer_c_prompt(h, w_in, lam_p, subln, w_out, table, lam_init):
    B, S, _ = h.shape
    q, k, v = _c_project(h, w_in)
    lam = _diff_lambda(lam_p, lam_init)
    nb = S // QBLOCK
    qb = q.reshape(B, nb, QBLOCK, C_KV_HEADS, C_GROUP, 2, C_HD).swapaxes(0, 1)
    kpos = jnp.arange(S, dtype=jnp.int32)

    def block(args):
        qi, bi = args
        qpos = bi * QBLOCK + jnp.arange(QBLOCK, dtype=jnp.int32)
        dist = qpos[:, None] - kpos[None, :]
        return _c_combine(_c_stats(qi, k, v, dist, dist >= 0, table), lam, subln, lam_init)

    o = lax.map(block, (qb, jnp.arange(nb, dtype=jnp.int32)))
    o = o.swapaxes(0, 1).reshape(B, S, C_HEADS * C_VD).astype(h.dtype)
    return o @ w_out, jnp.stack([k, v], axis=2)


def mixer_c_sample(h, w_in, lam_p, subln, w_out, table, lam_init, cache, ci, page_table):
    B, T, _ = h.shape
    q, k, v = _c_project(h, w_in)
    lam = _diff_lambda(lam_p, lam_init)
    n_pages = page_table.shape[1]
    qpos = n_pages * PAGE_SIZE + jnp.arange(T, dtype=jnp.int32)
    tt = jnp.arange(T, dtype=jnp.int32)
    dist_new = tt[:, None] - tt[None, :]
    stats = _c_stats(q, k, v, dist_new, dist_new >= 0, table)

    def step(carry, j):
        pages = cache[ci, page_table[:, j]]
        kpos = j * PAGE_SIZE + jnp.arange(PAGE_SIZE, dtype=jnp.int32)
        dist = qpos[:, None] - kpos[None, :]
        st = _c_stats(q, pages[:, :, 0], pages[:, :, 1], dist, dist >= 0, table)
        return _c_merge(carry, st), None

    stats, _ = lax.scan(step, stats, jnp.arange(n_pages, dtype=jnp.int32))
    o = _c_combine(stats, lam, subln, lam_init).reshape(B, T, C_HEADS * C_VD).astype(h.dtype)
    return o @ w_out, jnp.stack([k, v], axis=2)


def moe(x, w_grp, b_grp, w_rt, b_rt, w_gate, w_up, w_down):
    B, T, D = x.shape
    n = B * T
    xf = x.reshape(n, D)
    gl = (xf @ w_grp).astype(jnp.float32) + b_grp.astype(jnp.float32)
    g_sel = jnp.argmax(gl, axis=-1)
    g_oh = jax.nn.one_hot(g_sel, N_GROUPS, dtype=jnp.float32)
    g_prob = jnp.sum(jax.nn.softmax(gl, axis=-1) * g_oh, axis=-1, keepdims=True)
    el = jnp.einsum('nd,gde->nge', xf, w_rt).astype(jnp.float32) + b_rt.astype(jnp.float32)
    el = jnp.einsum('nge,ng->ne', el, g_oh)
    top_v, top_i = lax.top_k(el, TOP_K)
    wts = jax.nn.softmax(top_v, axis=-1) * g_prob
    eid = g_sel[:, None] * EXPERTS_PER_GROUP + top_i
    gates = jnp.sum(jax.nn.one_hot(eid, N_EXPERTS, dtype=jnp.float32) * wts[..., None], axis=1)
    chunk = min(MOE_CHUNK, n)
    n_chunks = -(-n // chunk)
    pad = n_chunks * chunk - n
    xc = jnp.pad(xf, ((0, pad), (0, 0))).reshape(n_chunks, chunk, D)
    gc = jnp.pad(gates, ((0, pad), (0, 0))).reshape(n_chunks, chunk, N_EXPERTS)

    def run(args):
        xi, gi = args
        hid = jax.nn.silu(jnp.einsum('nd,edf->nef', xi, w_gate)) * jnp.einsum('nd,edf->nef', xi, w_up)
        return jnp.einsum('nef,efd->nd', hid * gi[..., None].astype(hid.dtype), w_down)

    y = lax.map(run, (xc, gc)).reshape(n_chunks * chunk, D)[:n]
    return y.reshape(B, T, D)


def setup_inputs(seed: int = 0) -> dict:
    key = jax.random.key(seed)
    keys = iter(jax.random.split(key, 48))
    D = D_MODEL

    def nrm(shape, scale=1.0):
        return jax.random.normal(next(keys), shape, jnp.float32) * scale

    n_pages = PAST_LEN // PAGE_SIZE
    n_used = DEC_BATCH * n_pages
    n_phys = (5 * n_used + 3) // 4
    perm = jax.random.permutation(next(keys), n_phys)
    page_table = perm[:n_used].reshape(DEC_BATCH, n_pages).astype(jnp.int32)
    la = [min(w, PAST_LEN) for w in A_WINDOWS]
    lb = min(B_WINDOW, PAST_LEN)
    qkv_a = 3 * A_GROUPS * A_HEADS * A_HD
    qkv_b = (B_HEADS + 2 * B_KV_HEADS) * B_HD
    qkv_c = C_HEADS * 2 * C_HD + 2 * C_KV_HEADS * C_VD
    return {
        'x_prompt': nrm((BATCH, SEQ, D)),
        'x_sample': nrm((DEC_BATCH, DEC_SEQ, D)),
        'cache_a0_kv': nrm((N_A, DEC_BATCH, la[0], 2, A_HEADS, A_HD)),
        'cache_a1_kv': nrm((N_A, DEC_BATCH, la[1], 2, A_HEADS, A_HD)),
        'cache_a2_kv': nrm((N_A, DEC_BATCH, la[2], 2, A_HEADS, A_HD)),
        'cache_b_kv': nrm((N_B, DEC_BATCH, lb, 2, B_KV_HEADS, B_HD)),
        'cache_c_kv': nrm((N_C, n_phys, PAGE_SIZE, 2, C_KV_HEADS, C_VD)),
        'page_table': page_table,
        'c_prompt': nrm((BATCH, D)),
        'c_sample': nrm((DEC_BATCH, D)),
        'rel_table': nrm((NUM_BUCKETS, BIAS_HEADS), 0.5),
        'w_ada': nrm((DEPTH, D, 6 * D), 0.5 * D ** -0.5),
        'b_ada': nrm((DEPTH, 6 * D), 0.02),
        'g_mix': 1.0 + nrm((DEPTH, D), 0.05),
        'g_ffn': 1.0 + nrm((DEPTH, D), 0.05),
        'g_final': 1.0 + nrm((D,), 0.05),
        'w_in_a': nrm((N_A, D, qkv_a), D ** -0.5),
        'w_out_a': nrm((N_A, A_HEADS * A_HD, D), (A_HEADS * A_HD) ** -0.5),
        'w_in_b': nrm((N_B, D, qkv_b), D ** -0.5),
        'sinks_b': nrm((N_B, B_HEADS), 0.5),
        'w_out_b': nrm((N_B, B_HEADS * B_HD, D), (B_HEADS * B_HD) ** -0.5),
        'w_in_c': nrm((N_C, D, qkv_c), D ** -0.5),
        'lambda_c': nrm((N_C, 4, C_HD), 0.1),
        'subln_c': 1.0 + nrm((N_C, C_VD), 0.05),
        'w_out_c': nrm((N_C, C_HEADS * C_VD, D), (C_HEADS * C_VD) ** -0.5),
        'w_grp': nrm((DEPTH, D, N_GROUPS), D ** -0.5),
        'b_grp': nrm((DEPTH, N_GROUPS), 0.01),
        'w_rt': nrm((DEPTH, N_GROUPS, D, EXPERTS_PER_GROUP), D ** -0.5),
        'b_rt': nrm((DEPTH, N_GROUPS, EXPERTS_PER_GROUP), 0.01),
        'w_gate': nrm((DEPTH, N_EXPERTS, D, D_EXPERT), D ** -0.5),
        'w_up': nrm((DEPTH, N_EXPERTS, D, D_EXPERT), D ** -0.5),
        'w_down': nrm((DEPTH, N_EXPERTS, D_EXPERT, D), D_EXPERT ** -0.5),
    }


def reference(x_prompt, x_sample, cache_a0_kv, cache_a1_kv, cache_a2_kv, cache_b_kv, cache_c_kv,
              page_table, c_prompt, c_sample, rel_table, w_ada, b_ada, g_mix, g_ffn, g_final,
              w_in_a, w_out_a, w_in_b, sinks_b, w_out_b, w_in_c, lambda_c, subln_c, w_out_c,
              w_grp, b_grp, w_rt, b_rt, w_gate, w_up, w_down):
    xp, xs = x_prompt, x_sample
    a_caches = (cache_a0_kv, cache_a1_kv, cache_a2_kv)
    a_p = [[] for _ in range(A_GROUPS)]
    a_s = [[] for _ in range(A_GROUPS)]
    b_p, b_s, c_p, c_s = [], [], [], []
    for i in range(DEPTH):
        kind, li = i % N_MIXERS, i // N_MIXERS
        mp = ada(c_prompt, w_ada[i], b_ada[i])
        ms = ada(c_sample, w_ada[i], b_ada[i])
        hp = modulate(rmsnorm(xp, g_mix[i]), mp[0], mp[1])
        hs = modulate(rmsnorm(xs, g_mix[i]), ms[0], ms[1])
        if kind == 0:
            yp, stp = mixer_a_prompt(hp, w_in_a[li], w_out_a[li], rel_table)
            ys, sts = mixer_a_sample(hs, w_in_a[li], w_out_a[li], rel_table, [c[li] for c in a_caches])
            for g in range(A_GROUPS):
                a_p[g].append(stp[g])
                a_s[g].append(sts[g])
        elif kind == 1:
            yp, stp = mixer_b_prompt(hp, w_in_b[li], sinks_b[li], w_out_b[li], rel_table)
            ys, sts = mixer_b_sample(hs, w_in_b[li], sinks_b[li], w_out_b[li], rel_table, cache_b_kv[li])
            b_p.append(stp)
            b_s.append(sts)
        else:
            lam_init = 0.8 - 0.6 * math.exp(-0.3 * i)
            yp, stp = mixer_c_prompt(hp, w_in_c[li], lambda_c[li], subln_c[li], w_out_c[li], rel_table, lam_init)
            ys, sts = mixer_c_sample(hs, w_in_c[li], lambda_c[li], subln_c[li], w_out_c[li], rel_table,
                                     lam_init, cache_c_kv, li, page_table)
            c_p.append(stp)
            c_s.append(sts)
        xp = xp + mp[2] * yp
        xs = xs + ms[2] * ys
        hp = modulate(rmsnorm(xp, g_ffn[i]), mp[3], mp[4])
        hs = modulate(rmsnorm(xs, g_ffn[i]), ms[3], ms[4])
        xp = xp + mp[5] * moe(hp, w_grp[i], b_grp[i], w_rt[i], b_rt[i], w_gate[i], w_up[i], w_down[i])
        xs = xs + ms[5] * moe(hs, w_grp[i], b_grp[i], w_rt[i], b_rt[i], w_gate[i], w_up[i], w_down[i])
    y_prompt = rmsnorm(xp, g_final)
    y_sample = rmsnorm(xs, g_final)
    return (y_prompt, y_sample,
            jnp.stack(a_p[0]), jnp.stack(a_s[0]),
            jnp.stack(a_p[1]), jnp.stack(a_s[1]),
            jnp.stack(a_p[2]), jnp.stack(a_s[2]),
            jnp.stack(b_p), jnp.stack(b_s),
            jnp.stack(c_p), jnp.stack(c_s))
```

```python
import functools
import math

import jax
import jax.numpy as jnp
from jax import lax
from jax.experimental import pallas as pl
from jax.experimental.pallas import tpu as pltpu

F32 = jnp.float32
BF16 = jnp.bfloat16

N_MIXERS = 3
A_WINDOWS = (128, 512, 2048)
A_DILATIONS = (1, 4, 16)
A_GROUPS = 3
A_HEADS = 8
A_HD = 64
B_WINDOW = 128
B_HEADS = 16
B_KV_HEADS = 2
B_GROUP = B_HEADS // B_KV_HEADS
B_HD = 64
C_HEADS = 8
C_KV_HEADS = 4
C_GROUP = C_HEADS // C_KV_HEADS
C_HD = 64
C_VD = 2 * C_HD
NUM_BUCKETS = 32
MAX_DISTANCE = 2048
N_GROUPS = 4
EXPERTS_PER_GROUP = 8
N_EXPERTS = N_GROUPS * EXPERTS_PER_GROUP
PAGE_SIZE = 128
QBLOCK = 128
NORM_EPS = 1e-6
NEG = -1e30

LANES = 128
ROUTER_LANES = LANES
VMEM_LIMIT = 56 * 1024 * 1024


def _cparams(*sem):
    return pltpu.CompilerParams(dimension_semantics=sem, vmem_limit_bytes=VMEM_LIMIT)


def _largest_tile(n, cap, mult):
    best = None
    for t in range(mult, min(n, cap) + 1, mult):
        if n % t == 0:
            best = t
    assert best is not None, (n, cap, mult)
    return best


def _t5_bucket(dist):
    max_exact = NUM_BUCKETS // 2
    d = jnp.maximum(dist, 0)
    ratio = jnp.log(jnp.maximum(d, 1).astype(F32) / max_exact) / math.log(MAX_DISTANCE / max_exact)
    large = jnp.minimum(max_exact + (ratio * (NUM_BUCKETS - max_exact)).astype(jnp.int32), NUM_BUCKETS - 1)
    return jnp.where(d < max_exact, d, large)


def _rel_bias(table, dist):
    return table.astype(F32)[_t5_bucket(dist)]


def _band_bias(table, dil, head_cols):
    i = jnp.arange(QBLOCK, dtype=jnp.int32)[:, None]
    j = jnp.arange(2 * QBLOCK, dtype=jnp.int32)[None, :]
    m = QBLOCK + i - j
    ok = (m >= 0) & (m <= QBLOCK)
    b = _rel_bias(table, dil * m)[..., head_cols]
    return jnp.where(ok[None], b.transpose(2, 0, 1), NEG)


def _ada_kernel(c_ref, w_ref, b_ref, o_ref):
    c = c_ref[...]
    a = c * jax.nn.sigmoid(c)
    o_ref[...] = jnp.dot(a.astype(BF16), w_ref[...].astype(BF16), preferred_element_type=F32) + b_ref[...]


def _ada_all(c_all, w_ada, b_ada):
    depth, d, f = w_ada.shape
    n = c_all.shape[0]
    tn = _largest_tile(f, 1024, LANES)
    return pl.pallas_call(
        _ada_kernel,
        grid=(depth, f // tn),
        in_specs=[pl.BlockSpec((n, d), lambda i, j: (0, 0)),
                  pl.BlockSpec((None, d, tn), lambda i, j: (i, 0, j)),
                  pl.BlockSpec((None, 1, tn), lambda i, j: (i, 0, j))],
        out_specs=pl.BlockSpec((None, n, tn), lambda i, j: (i, 0, j)),
        out_shape=jax.ShapeDtypeStruct((depth, n, f), F32),
        compiler_params=_cparams("parallel", "parallel"),
        name="ada",
    )(c_all, w_ada, b_ada.reshape(depth, 1, f))


def _mod_spec(mod, layer, chunk, tm, width):
    per_row = mod.shape[2] != 1
    rows = tm if per_row else 1
    per = mod.shape[3] // 6 // width
    return pl.BlockSpec((None, None, rows, width),
                        lambda b, i, *_: (layer, b, i if per_row else 0, chunk * per))


def _norm_mod(x, g, shift, scale):
    y = x * lax.rsqrt(jnp.mean(x * x, axis=-1, keepdims=True) + NORM_EPS) * g
    return y * (1.0 + scale) + shift


def _nmm_kernel(x_ref, g_ref, sh_ref, sc_ref, w_ref, o_ref, h_ref):
    @pl.when(pl.program_id(2) == 0)
    def _():
        h_ref[...] = _norm_mod(x_ref[...], g_ref[...], sh_ref[...], sc_ref[...]).astype(h_ref.dtype)

    o_ref[...] = jnp.dot(h_ref[...], w_ref[...], preferred_element_type=F32)


def _norm_mod_matmul(x, g, mod, layer, w):
    nb, r, d = x.shape
    f = w.shape[1]
    tm = _largest_tile(r, 512, 8)
    tn = _largest_tile(f, 1536, LANES)
    return pl.pallas_call(
        _nmm_kernel,
        grid=(nb, r // tm, f // tn),
        in_specs=[pl.BlockSpec((None, tm, d), lambda b, i, j: (b, i, 0)),
                  pl.BlockSpec((None, 1, d), lambda b, i, j: (layer, 0, 0)),
                  _mod_spec(mod, layer, 0, tm, d),
                  _mod_spec(mod, layer, 1, tm, d),
                  pl.BlockSpec((d, tn), lambda b, i, j: (0, j))],
        out_specs=pl.BlockSpec((None, tm, tn), lambda b, i, j: (b, i, j)),
        out_shape=jax.ShapeDtypeStruct((nb, r, f), F32),
        scratch_shapes=[pltpu.VMEM((tm, d), BF16)],
        compiler_params=_cparams("parallel", "parallel", "arbitrary"),
        name="norm_qkv",
    )(x, g, mod, mod, w)


def _proj_res_kernel(o_ref, w_ref, x_ref, gate_ref, out_ref):
    y = jnp.dot(o_ref[...].astype(BF16), w_ref[...], preferred_element_type=F32)
    out_ref[...] = x_ref[...] + gate_ref[...] * y


def _merge_proj_res_kernel(o0, o1, o2, l0, l1, l2, w_ref, x_ref, gate_ref, out_ref):
    a0, a1, a2 = l0[...], l1[...], l2[...]
    m = jnp.maximum(jnp.maximum(a0, a1), a2)
    e0, e1, e2 = jnp.exp(a0 - m), jnp.exp(a1 - m), jnp.exp(a2 - m)
    o = (e0 * o0[...] + e1 * o1[...] + e2 * o2[...]) / (e0 + e1 + e2)
    y = jnp.dot(o.astype(BF16), w_ref[...], preferred_element_type=F32)
    out_ref[...] = x_ref[...] + gate_ref[...] * y


def _proj_res(os_, lses, w, x, mod, layer):
    nb, r, d = x.shape
    k = w.shape[0]
    tm = _largest_tile(r, 512, 8)
    ospec = pl.BlockSpec((None, tm, k), lambda b, i: (b, i, 0))
    xspec = pl.BlockSpec((None, tm, d), lambda b, i: (b, i, 0))
    ins = list(os_) + list(lses)
    kern = _merge_proj_res_kernel if lses else _proj_res_kernel
    return pl.pallas_call(
        kern,
        grid=(nb, r // tm),
        in_specs=[ospec] * len(ins) + [pl.BlockSpec((k, d), lambda b, i: (0, 0)), xspec,
                                       _mod_spec(mod, layer, 2, tm, d)],
        out_specs=xspec,
        out_shape=jax.ShapeDtypeStruct((nb, r, d), F32),
        compiler_params=_cparams("parallel", "parallel"),
        name="proj_res",
    )(*ins, w, x, mod)


def _a_prompt_kernel(q_ref, kp_ref, kc_ref, vp_ref, vc_ref, bias_ref, o_ref, l_ref):
    bi = pl.program_id(2)
    scale = A_HD ** -0.5
    q = q_ref[...].astype(BF16)
    k = jnp.concatenate([kp_ref[...], kc_ref[...]], axis=0).astype(BF16)
    v = jnp.concatenate([vp_ref[...], vc_ref[...]], axis=0).astype(BF16)
    col = lax.broadcasted_iota(jnp.int32, (QBLOCK, 2 * QBLOCK), 1)
    keep = (col >= QBLOCK) | (bi > 0)
    outs, lses = [], []
    for h in range(A_HEADS):
        sl = slice(h * A_HD, (h + 1) * A_HD)
        s = lax.dot_general(q[:, sl], k[:, sl], (((1,), (1,)), ((), ())), preferred_element_type=F32)
        s = jnp.where(keep, s * scale + bias_ref[h], NEG)
        m = jnp.max(s, axis=-1, keepdims=True)
        p = jnp.exp(s - m)
        l = jnp.sum(p, axis=-1, keepdims=True)
        outs.append(jnp.dot(p.astype(BF16), v[:, sl], preferred_element_type=F32) / l)
        lses.append(jnp.broadcast_to(m + jnp.log(l), (QBLOCK, A_HD)))
    o_ref[...] = jnp.concatenate(outs, axis=1)
    l_ref[...] = jnp.concatenate(lses, axis=1)


def _a_prompt_group(qkv, g, bias):
    b, t, f = qkv.shape
    dil = A_DILATIONS[g]
    lr = t // dil
    nq = lr // QBLOCK
    w = A_HEADS * A_HD
    per = f // w
    view = qkv.reshape(b, lr, dil * f)

    def spec(sec, prev):
        return pl.BlockSpec((None, QBLOCK, w),
                            lambda bb, c, i: (bb, jnp.maximum(i - 1, 0) if prev else i, c * per + sec * A_GROUPS + g))

    ospec = pl.BlockSpec((None, QBLOCK, w), lambda bb, c, i: (bb, i, c))
    oshape = jax.ShapeDtypeStruct((b, lr, dil * w), F32)
    o, l = pl.pallas_call(
        _a_prompt_kernel,
        grid=(b, dil, nq),
        in_specs=[spec(0, False), spec(1, True), spec(1, False), spec(2, True), spec(2, False),
                  pl.BlockSpec((A_HEADS, QBLOCK, 2 * QBLOCK), lambda bb, c, i: (0, 0, 0))],
        out_specs=[ospec, ospec],
        out_shape=[oshape, oshape],
        compiler_params=_cparams("parallel", "parallel", "parallel"),
        name=f"a_prompt_g{g}",
    )(view, view, view, view, view, bias)
    return o.reshape(b, t, w), l.reshape(b, t, w)


def _decode_head(kt, vt, q, kn, vn, bias, bias_new, scale, sink=None):
    s = jnp.sum(kt * q, axis=1, keepdims=True) * scale + bias
    s_new = jnp.sum(kn * q, axis=1, keepdims=True) * scale + bias_new
    m = jnp.maximum(jnp.max(s, axis=-1, keepdims=True), s_new)
    if sink is not None:
        m = jnp.maximum(m, sink)
    p = jnp.exp(s - m)
    pn = jnp.exp(s_new - m)
    l = jnp.sum(p, axis=-1, keepdims=True) + pn
    if sink is not None:
        l = l + jnp.exp(sink - m)
    o = (jnp.sum(vt * p, axis=-1, keepdims=True) + pn * vn) / l
    return o, m + jnp.log(l)


def _a_sample_kernel(q_ref, kvn_ref, c_ref, bias_ref, bn_ref, *rest):
    o_ref, l_ref, out_ref = rest[-3:]
    L = c_ref.shape[-1]
    scale = A_HD ** -0.5
    last = lax.broadcasted_iota(jnp.int32, (1, 1, L), 2) == L - 1
    for h in range(A_HEADS):
        kt, vt = c_ref[:, 0, h], c_ref[:, 1, h]
        kn, vn = kvn_ref[:, 0, h], kvn_ref[:, 1, h]
        o, lse = _decode_head(kt, vt, q_ref[:, h], kn, vn, bias_ref[h], bn_ref[h], scale)
        o_ref[:, h] = o
        l_ref[:, h] = lse
        out_ref[:, 0, h] = jnp.where(last, kn, pltpu.roll(kt, L - 1, 2))
        out_ref[:, 1, h] = jnp.where(last, vn, pltpu.roll(vt, L - 1, 2))


def _a_sample_group(qkv, cache_t, prev_out, li, g, table):
    n, f = qkv.shape
    dil = A_DILATIONS[g]
    n_lay, nb, two, hh, hd, L = cache_t.shape
    assert (nb, two, hh, hd) == (n, 2, A_HEADS, A_HD) and L == A_WINDOWS[g]
    w = A_HEADS * A_HD
    col = lambda sec: qkv[:, (sec * A_GROUPS + g) * w:(sec * A_GROUPS + g + 1) * w]
    q = col(0).reshape(n, A_HEADS, A_HD, 1)
    kvn = jnp.stack([col(1), col(2)], axis=1).reshape(n, 2, A_HEADS, A_HD, 1)
    dist = L - jnp.arange(L, dtype=jnp.int32)
    bias = jnp.where((dist % dil == 0)[None], _rel_bias(table, dist).T, NEG).reshape(A_HEADS, 1, L)
    bn = _rel_bias(table, jnp.zeros((1,), jnp.int32)).T.reshape(A_HEADS, 1, 1)
    bt = max(1, (4 * 1024 * 1024) // (2 * w * L * 4))
    cspec = pl.BlockSpec((None, bt, 2, A_HEADS, A_HD, L), lambda b: (li, b, 0, 0, 0, 0))
    in_specs = [pl.BlockSpec((bt, A_HEADS, A_HD, 1), lambda b: (b, 0, 0, 0)),
                pl.BlockSpec((bt, 2, A_HEADS, A_HD, 1), lambda b: (b, 0, 0, 0, 0)),
                cspec,
                pl.BlockSpec((A_HEADS, 1, L), lambda b: (0, 0, 0)),
                pl.BlockSpec((A_HEADS, 1, 1), lambda b: (0, 0, 0))]
    args = [q, kvn, cache_t, bias, bn]
    aliases = {}
    if prev_out is not None:
        in_specs.append(pl.BlockSpec(memory_space=pl.ANY))
        args.append(prev_out)
        aliases = {len(args) - 1: 2}
    o, lse, out = pl.pallas_call(
        _a_sample_kernel,
        grid=(n // bt,),
        in_specs=in_specs,
        out_specs=[pl.BlockSpec((bt, A_HEADS, A_HD, 1), lambda b: (b, 0, 0, 0)),
                   pl.BlockSpec((bt, A_HEADS, 1, 1), lambda b: (b, 0, 0, 0)),
                   cspec],
        out_shape=[jax.ShapeDtypeStruct((n, A_HEADS, A_HD, 1), F32),
                   jax.ShapeDtypeStruct((n, A_HEADS, 1, 1), F32),
                   jax.ShapeDtypeStruct(cache_t.shape, F32)],
        input_output_aliases=aliases,
        compiler_params=_cparams("parallel"),
        name=f"a_sample_g{g}",
    )(*args)
    lse = jnp.broadcast_to(lse.reshape(n, A_HEADS, 1), (n, A_HEADS, A_HD)).reshape(n, w)
    return o.reshape(n, w), lse, out


def _head_mask(rows, width, hd, lane_head_of_row):
    lane = lax.broadcasted_iota(jnp.int32, (rows, width), 1) // hd
    row = lax.broadcasted_iota(jnp.int32, (rows, width), 0)
    return (lane == lane_head_of_row(row)).astype(F32)


def _b_prompt_kernel(q_ref, kp_ref, kc_ref, vp_ref, vc_ref, bias_ref, sink_ref, o_ref):
    bi = pl.program_id(1)
    scale = B_HD ** -0.5
    q = q_ref[...].astype(BF16)
    k = jnp.concatenate([kp_ref[...], kc_ref[...]], axis=0).astype(BF16)
    v = jnp.concatenate([vp_ref[...], vc_ref[...]], axis=0).astype(BF16)
    col = lax.broadcasted_iota(jnp.int32, (B_GROUP * QBLOCK, 2 * QBLOCK), 1)
    keep = (col >= QBLOCK) | (bi > 0)
    outs = []
    for kh in range(B_KV_HEADS):
        qs = jnp.concatenate([q[:, (kh * B_GROUP + g) * B_HD:(kh * B_GROUP + g + 1) * B_HD]
                              for g in range(B_GROUP)], axis=0)
        sl = slice(kh * B_HD, (kh + 1) * B_HD)
        s = lax.dot_general(qs, k[:, sl], (((1,), (1,)), ((), ())), preferred_element_type=F32)
        s = jnp.where(keep, s * scale + bias_ref[kh], NEG)
        sink = sink_ref[kh]
        m = jnp.maximum(jnp.max(s, axis=-1, keepdims=True), sink)
        p = jnp.exp(s - m)
        den = jnp.sum(p, axis=-1, keepdims=True) + jnp.exp(sink - m)
        o = jnp.dot(p.astype(BF16), v[:, sl], preferred_element_type=F32) / den
        outs += [o[g * QBLOCK:(g + 1) * QBLOCK] for g in range(B_GROUP)]
    o_ref[...] = jnp.concatenate(outs, axis=1)


def _b_prompt(qkv, sinks, table):
    b, t, f = qkv.shape
    nq = t // QBLOCK
    wq = B_HEADS * B_HD
    wk = B_KV_HEADS * B_HD
    head_cols = jnp.arange(B_HEADS) // (B_HEADS // table.shape[1])
    bias = _band_bias(table, 1, head_cols).reshape(B_KV_HEADS, B_GROUP * QBLOCK, 2 * QBLOCK)
    sink = jnp.broadcast_to(sinks.astype(F32).reshape(B_KV_HEADS, B_GROUP, 1, 1),
                            (B_KV_HEADS, B_GROUP, QBLOCK, 1)).reshape(B_KV_HEADS, B_GROUP * QBLOCK, 1)

    def kvspec(colblock, prev):
        return pl.BlockSpec((None, QBLOCK, wk), lambda bb, i: (bb, jnp.maximum(i - 1, 0) if prev else i, colblock))

    kcol = wq // wk
    return pl.pallas_call(
        _b_prompt_kernel,
        grid=(b, nq),
        in_specs=[pl.BlockSpec((None, QBLOCK, wq), lambda bb, i: (bb, i, 0)),
                  kvspec(kcol, True), kvspec(kcol, False), kvspec(kcol + 1, True), kvspec(kcol + 1, False),
                  pl.BlockSpec(bias.shape, lambda bb, i: (0, 0, 0)),
                  pl.BlockSpec(sink.shape, lambda bb, i: (0, 0, 0))],
        out_specs=pl.BlockSpec((None, QBLOCK, wq), lambda bb, i: (bb, i, 0)),
        out_shape=jax.ShapeDtypeStruct((b, t, wq), F32),
        compiler_params=_cparams("parallel", "parallel"),
        name="b_prompt",
    )(qkv, qkv, qkv, qkv, qkv, bias, sink)


def _b_sample_kernel(q_ref, kvn_ref, c_ref, bias_ref, bn_ref, sink_ref, o_ref):
    scale = B_HD ** -0.5
    for hq in range(B_HEADS):
        kh = hq // B_GROUP
        o, _ = _decode_head(c_ref[:, 0, kh], c_ref[:, 1, kh], q_ref[:, hq], kvn_ref[:, 0, kh], kvn_ref[:, 1, kh],
                            bias_ref[hq], bn_ref[hq], scale, sink_ref[hq])
        o_ref[:, hq] = o


def _b_sample(qkv, cache_t, li, sinks, table):
    n, f = qkv.shape
    n_lay, nb, two, kvh, hd, L = cache_t.shape
    assert (nb, two, kvh, hd) == (n, 2, B_KV_HEADS, B_HD) and L == B_WINDOW
    wq = B_HEADS * B_HD
    wk = B_KV_HEADS * B_HD
    head_cols = jnp.arange(B_HEADS) // (B_HEADS // table.shape[1])
    dist = L - jnp.arange(L, dtype=jnp.int32)
    bias = _rel_bias(table, dist)[:, head_cols].T.reshape(B_HEADS, 1, L)
    bn = _rel_bias(table, jnp.zeros((1,), jnp.int32))[:, head_cols].T.reshape(B_HEADS, 1, 1)
    q = qkv[:, :wq].reshape(n, B_HEADS, B_HD, 1)
    kvn = jnp.stack([qkv[:, wq:wq + wk], qkv[:, wq + wk:]], axis=1).reshape(n, 2, B_KV_HEADS, B_HD, 1)
    bt = 8
    o = pl.pallas_call(
        _b_sample_kernel,
        grid=(n // bt,),
        in_specs=[pl.BlockSpec((bt, B_HEADS, B_HD, 1), lambda b: (b, 0, 0, 0)),
                  pl.BlockSpec((bt, 2, B_KV_HEADS, B_HD, 1), lambda b: (b, 0, 0, 0, 0)),
                  pl.BlockSpec((None, bt, 2, B_KV_HEADS, B_HD, L), lambda b: (li, b, 0, 0, 0, 0)),
                  pl.BlockSpec((B_HEADS, 1, L), lambda b: (0, 0, 0)),
                  pl.BlockSpec((B_HEADS, 1, 1), lambda b: (0, 0, 0)),
                  pl.BlockSpec((B_HEADS, 1, 1), lambda b: (0, 0, 0))],
        out_specs=pl.BlockSpec((bt, B_HEADS, B_HD, 1), lambda b: (b, 0, 0, 0)),
        out_shape=jax.ShapeDtypeStruct((n, B_HEADS, B_HD, 1), F32),
        compiler_params=_cparams("parallel"),
        name="b_sample",
    )(q, kvn, cache_t, bias, bn, sinks.astype(F32).reshape(B_HEADS, 1, 1))
    return o.reshape(n, wq)


def _diff_lambda(lp, lam_init):
    return (jnp.exp(jnp.sum(lp[0:1] * lp[1:2], axis=-1, keepdims=True))
            - jnp.exp(jnp.sum(lp[2:3] * lp[3:4], axis=-1, keepdims=True)) + lam_init)


def _diff_out(o1, o2, lam, subln, lam_init):
    d = o1 - lam * o2
    y = d * lax.rsqrt(jnp.mean(d * d, axis=-1, keepdims=True) + NORM_EPS) * subln
    return y * (1.0 - lam_init)


C_BLOCK = 256


def _c_prompt_kernel(qi_tab, ki_tab, q_ref, k_ref, v_ref, bias_ref, lam_ref, subln_ref, o_ref,
                     m_sc, l_sc, acc_sc, *, lam_init):
    t = pl.program_id(2)
    qi = qi_tab[t]
    ki = ki_tab[t]
    scale = C_HD ** -0.5

    @pl.when(ki == 0)
    def _():
        m_sc[...] = jnp.full(m_sc.shape, NEG, F32)
        l_sc[...] = jnp.zeros(l_sc.shape, F32)
        acc_sc[...] = jnp.zeros(acc_sc.shape, F32)

    q = q_ref[...].astype(BF16)
    k = k_ref[...].astype(BF16)
    v = v_ref[...].astype(BF16)
    for g in range(C_GROUP):
        bias = bias_ref[g]
        for mp in range(2):
            idx = g * 2 + mp
            qs = q[:, g * C_VD + mp * C_HD:g * C_VD + (mp + 1) * C_HD]
            ks = k[:, mp * C_HD:(mp + 1) * C_HD]
            s = lax.dot_general(qs, ks, (((1,), (1,)), ((), ())), preferred_element_type=F32) * scale + bias
            m_old = m_sc[idx]
            m_new = jnp.maximum(m_old, jnp.max(s, axis=-1, keepdims=True))
            alpha = jnp.exp(m_old - m_new)
            p = jnp.exp(s - m_new)
            l_sc[idx] = alpha * l_sc[idx] + jnp.sum(p, axis=-1, keepdims=True)
            acc_sc[idx] = alpha * acc_sc[idx] + jnp.dot(p.astype(BF16), v, preferred_element_type=F32)
            m_sc[idx] = m_new

    @pl.when(ki == qi)
    def _():
        lam = _diff_lambda(lam_ref[...], lam_init)
        outs = []
        for g in range(C_GROUP):
            o1 = acc_sc[g * 2] / l_sc[g * 2]
            o2 = acc_sc[g * 2 + 1] / l_sc[g * 2 + 1]
            outs.append(_diff_out(o1, o2, lam, subln_ref[...], lam_init))
        o_ref[...] = jnp.concatenate(outs, axis=1)


def _c_prompt(qkv, lam_p, subln, table, lam_init):
    b, t, f = qkv.shape
    blk = C_BLOCK
    nblk = t // blk
    pairs = [(qi, ki) for qi in range(nblk) for ki in range(qi + 1)]
    qi_tab = jnp.asarray([p[0] for p in pairs], jnp.int32)
    ki_tab = jnp.asarray([p[1] for p in pairs], jnp.int32)
    i = jnp.arange(blk, dtype=jnp.int32)[:, None]
    j = jnp.arange(blk, dtype=jnp.int32)[None, :]
    dist = jnp.arange(nblk, dtype=jnp.int32)[:, None, None] * blk + i - j
    bias = jnp.where((dist >= 0)[:, None], _rel_bias(table, dist).transpose(0, 3, 1, 2), NEG)
    wq = C_GROUP * C_VD
    qcols = C_HEADS * C_VD
    grid_spec = pltpu.PrefetchScalarGridSpec(
        num_scalar_prefetch=2,
        grid=(b, C_KV_HEADS, len(pairs)),
        in_specs=[pl.BlockSpec((None, blk, wq), lambda bb, kh, tt, qt, kt: (bb, qt[tt], kh)),
                  pl.BlockSpec((None, blk, C_VD), lambda bb, kh, tt, qt, kt: (bb, kt[tt], qcols // C_VD + kh)),
                  pl.BlockSpec((None, blk, C_VD),
                               lambda bb, kh, tt, qt, kt: (bb, kt[tt], qcols // C_VD + C_KV_HEADS + kh)),
                  pl.BlockSpec((None, C_GROUP, blk, blk), lambda bb, kh, tt, qt, kt: (qt[tt] - kt[tt], kh, 0, 0)),
                  pl.BlockSpec((4, C_HD), lambda bb, kh, tt, qt, kt: (0, 0)),
                  pl.BlockSpec((1, C_VD), lambda bb, kh, tt, qt, kt: (0, 0))],
        out_specs=pl.BlockSpec((None, blk, wq), lambda bb, kh, tt, qt, kt: (bb, qt[tt], kh)),
        scratch_shapes=[pltpu.VMEM((2 * C_GROUP, blk, 1), F32), pltpu.VMEM((2 * C_GROUP, blk, 1), F32),
                        pltpu.VMEM((2 * C_GROUP, blk, C_VD), F32)],
    )
    return pl.pallas_call(
        functools.partial(_c_prompt_kernel, lam_init=lam_init),
        grid_spec=grid_spec,
        out_shape=jax.ShapeDtypeStruct((b, t, qcols), F32),
        compiler_params=_cparams("parallel", "parallel", "arbitrary"),
        name="c_prompt",
    )(qi_tab, ki_tab, qkv, qkv, qkv, bias, lam_p.astype(F32), subln.astype(F32).reshape(1, C_VD))


C_PAGES_PER_STEP = 8
C_MAPS = 2 * C_HEADS
C_ROWS = 2 * C_KV_HEADS


def _c_sample_kernel(pt_ref, q_ref, kvn_ref, lam_ref, subln_ref, bias_ref, bn_ref, *rest, lam_init):
    pages = rest[:C_PAGES_PER_STEP]
    o_ref, m_sc, l_sc, acc_sc = rest[C_PAGES_PER_STEP:]
    j = pl.program_id(1)
    scale = C_HD ** -0.5
    hm = _head_mask(C_MAPS, C_VD, C_HD, lambda r: r // C_HEADS)
    qe = jnp.concatenate([q_ref[...]] * 2, axis=-1) * hm
    qb = qe.astype(BF16)
    row_kh = (lax.broadcasted_iota(jnp.int32, (C_MAPS, 1), 0) % C_HEADS) // C_GROUP

    @pl.when(j == 0)
    def _():
        kvn = kvn_ref[...]
        s_new = jnp.zeros((C_MAPS, 1), F32)
        v_new = jnp.zeros((C_MAPS, C_VD), F32)
        for kh in range(C_KV_HEADS):
            s_kh = jnp.sum(qe * kvn[kh:kh + 1], axis=-1, keepdims=True)
            s_new = jnp.where(row_kh == kh, s_kh, s_new)
            v_new = jnp.where(row_kh == kh, kvn[C_KV_HEADS + kh:C_KV_HEADS + kh + 1], v_new)
        m_sc[...] = s_new * scale + bn_ref[...]
        l_sc[...] = jnp.ones(l_sc.shape, F32)
        acc_sc[...] = v_new

    m = m_sc[...]
    l = l_sc[...]
    acc = acc_sc[...]
    for pi in range(C_PAGES_PER_STEP):
        rows = pages[pi][...].astype(BF16)
        s = lax.dot_general(qb, rows, (((1,), (1,)), ((), ())), preferred_element_type=F32) * scale + bias_ref[pi]
        m_new = jnp.maximum(m, jnp.max(s, axis=-1, keepdims=True))
        alpha = jnp.exp(m - m_new)
        p = jnp.exp(s - m_new)
        l = alpha * l + jnp.sum(p, axis=-1, keepdims=True)
        pv = pltpu.roll(p, C_KV_HEADS, 1)
        acc = alpha * acc + jnp.dot(pv.astype(BF16), rows, preferred_element_type=F32)
        m = m_new
    m_sc[...] = m
    l_sc[...] = l
    acc_sc[...] = acc

    @pl.when(j == pl.num_programs(1) - 1)
    def _():
        o = acc / l
        lam = _diff_lambda(lam_ref[...], lam_init)
        o_ref[...] = _diff_out(o[:C_HEADS], o[C_HEADS:], lam, subln_ref[...], lam_init)


def _c_sample(qkv, cache, ci, page_table, lam_p, subln, table, lam_init):
    n, f = qkv.shape
    n_lay, n_phys, page, two, kvh, vd = cache.shape
    assert (page, two, kvh, vd) == (PAGE_SIZE, 2, C_KV_HEADS, C_VD)
    n_pages = page_table.shape[1]
    pps = C_PAGES_PER_STEP
    assert n_pages % pps == 0
    qcols = C_HEADS * C_VD
    prow = page * C_ROWS
    view = cache.reshape(n_lay, n_phys, prow, C_VD)
    q16 = qkv[:, :qcols].reshape(n, C_HEADS, 2, C_HD).transpose(0, 2, 1, 3).reshape(n, C_MAPS, C_HD)
    kvn = qkv[:, qcols:].reshape(n, C_ROWS, C_VD)
    n_keys = n_pages * page
    b8 = _rel_bias(table, n_keys - jnp.arange(n_keys, dtype=jnp.int32)).T
    b16 = jnp.concatenate([b8, b8], axis=0)
    row_kh = (jnp.arange(C_MAPS) % C_HEADS) // C_GROUP
    own = row_kh[:, None, None] == jnp.arange(C_ROWS)[None, None, :]
    bias = jnp.where(own, b16[:, :, None], NEG).reshape(C_MAPS, n_pages, prow).transpose(1, 0, 2)
    bn = _rel_bias(table, jnp.zeros((1,), jnp.int32)).T
    bn = jnp.concatenate([bn, bn], axis=0)
    pt_flat = page_table.reshape(-1).astype(jnp.int32)

    def page_spec(pi):
        return pl.BlockSpec((None, None, prow, C_VD),
                            lambda b, j, pt: (ci, pt[b * n_pages + j * pps + pi], 0, 0))

    grid_spec = pltpu.PrefetchScalarGridSpec(
        num_scalar_prefetch=1,
        grid=(n, n_pages // pps),
        in_specs=[pl.BlockSpec((None, C_MAPS, C_HD), lambda b, j, pt: (b, 0, 0)),
                  pl.BlockSpec((None, C_ROWS, C_VD), lambda b, j, pt: (b, 0, 0)),
                  pl.BlockSpec((4, C_HD), lambda b, j, pt: (0, 0)),
                  pl.BlockSpec((1, C_VD), lambda b, j, pt: (0, 0)),
                  pl.BlockSpec((pps, C_MAPS, prow), lambda b, j, pt: (j, 0, 0)),
                  pl.BlockSpec((C_MAPS, 1), lambda b, j, pt: (0, 0))] + [page_spec(pi) for pi in range(pps)],
        out_specs=pl.BlockSpec((None, C_HEADS, C_VD), lambda b, j, pt: (b, 0, 0)),
        scratch_shapes=[pltpu.VMEM((C_MAPS, 1), F32), pltpu.VMEM((C_MAPS, 1), F32), pltpu.VMEM((C_MAPS, C_VD), F32)],
    )
    o = pl.pallas_call(
        functools.partial(_c_sample_kernel, lam_init=lam_init),
        grid_spec=grid_spec,
        out_shape=jax.ShapeDtypeStruct((n, C_HEADS, C_VD), F32),
        compiler_params=_cparams("parallel", "arbitrary"),
        name="c_sample",
    )(pt_flat, q16, kvn, lam_p.astype(F32), subln.astype(F32).reshape(1, C_VD), bias, bn, *([view] * pps))
    return o.reshape(n, qcols)


def _router_kernel(x_ref, g_ref, sh_ref, sc_ref, wr_ref, br_ref, h_ref, gates_ref):
    h = _norm_mod(x_ref[...], g_ref[...], sh_ref[...], sc_ref[...])
    h_ref[...] = h.astype(h_ref.dtype)
    logits = jnp.dot(h, wr_ref[...], preferred_element_type=F32, precision=lax.Precision.HIGHEST) + br_ref[...]
    lane = lax.broadcasted_iota(jnp.int32, logits.shape, 1).astype(F32)
    first = lambda hit: jnp.min(jnp.where(hit, lane, float(ROUTER_LANES)), axis=-1, keepdims=True)
    is_g = lane < N_GROUPS
    gl = jnp.where(is_g, logits, NEG)
    gmax = jnp.max(gl, axis=-1, keepdims=True)
    g_sel = first(is_g & (gl == gmax))
    g_prob = 1.0 / jnp.sum(jnp.where(is_g, jnp.exp(gl - gmax), 0.0), axis=-1, keepdims=True)
    lo = N_GROUPS + EXPERTS_PER_GROUP * g_sel
    in_grp = (lane >= lo) & (lane < lo + EXPERTS_PER_GROUP)
    el = jnp.where(in_grp, logits, NEG)
    v1 = jnp.max(el, axis=-1, keepdims=True)
    i1 = first(in_grp & (el == v1))
    rest = in_grp & (lane != i1)
    el2 = jnp.where(rest, el, NEG)
    v2 = jnp.max(el2, axis=-1, keepdims=True)
    i2 = first(rest & (el2 == v2))
    e2 = jnp.exp(v2 - v1)
    w1 = g_prob / (1.0 + e2)
    w2 = g_prob * e2 / (1.0 + e2)
    gates_ref[...] = jnp.where(lane == i1, w1, 0.0) + jnp.where(lane == i2, w2, 0.0)


def _router(x, g, mod, layer, wr, br):
    nb, r, d = x.shape
    tm = _largest_tile(r, 512, 8)
    xspec = pl.BlockSpec((None, tm, d), lambda b, i: (b, i, 0))
    return pl.pallas_call(
        _router_kernel,
        grid=(nb, r // tm),
        in_specs=[xspec, pl.BlockSpec((None, 1, d), lambda b, i: (layer, 0, 0)),
                  _mod_spec(mod, layer, 3, tm, d), _mod_spec(mod, layer, 4, tm, d),
                  pl.BlockSpec((None, d, ROUTER_LANES), lambda b, i: (layer, 0, 0)),
                  pl.BlockSpec((None, 1, ROUTER_LANES), lambda b, i: (layer, 0, 0))],
        out_specs=[xspec, pl.BlockSpec((None, tm, ROUTER_LANES), lambda b, i: (b, i, 0))],
        out_shape=[jax.ShapeDtypeStruct((nb, r, d), BF16), jax.ShapeDtypeStruct((nb, r, ROUTER_LANES), F32)],
        compiler_params=_cparams("parallel", "parallel"),
        name="router",
    )(x, g, mod, mod, wr, br)


def _moe_kernel(h_ref, gates_ref, wg_ref, wu_ref, wd_ref, x_ref, gate_ref, gf_ref, out_ref, acc_ref, *, final_norm):
    e = pl.program_id(2)

    @pl.when(e == 0)
    def _():
        acc_ref[...] = jnp.zeros(acc_ref.shape, F32)

    h = h_ref[...]
    a = jnp.dot(h, wg_ref[...], preferred_element_type=F32)
    u = jnp.dot(h, wu_ref[...], preferred_element_type=F32)
    gates = gates_ref[...]
    lane = lax.broadcasted_iota(jnp.int32, gates.shape, 1)
    ge = jnp.sum(jnp.where(lane == e + N_GROUPS, gates, 0.0), axis=-1, keepdims=True)
    hid = (a * jax.nn.sigmoid(a)) * u * ge
    acc_ref[...] += jnp.dot(hid.astype(BF16), wd_ref[...], preferred_element_type=F32)

    @pl.when(e == pl.num_programs(2) - 1)
    def _():
        y = x_ref[...] + gate_ref[...] * acc_ref[...]
        if final_norm:
            y = y * lax.rsqrt(jnp.mean(y * y, axis=-1, keepdims=True) + NORM_EPS) * gf_ref[...]
        out_ref[...] = y


def _moe(h, gates, wg, wu, wd, x, mod, layer, g_final, final_norm):
    nb, r, d = x.shape
    de = wg.shape[3]
    tm = _largest_tile(r, 1024, 8)
    xspec = pl.BlockSpec((None, tm, d), lambda b, i, e: (b, i, 0))
    return pl.pallas_call(
        functools.partial(_moe_kernel, final_norm=final_norm),
        grid=(nb, r // tm, N_EXPERTS),
        in_specs=[xspec, pl.BlockSpec((None, tm, ROUTER_LANES), lambda b, i, e: (b, i, 0)),
                  pl.BlockSpec((None, None, d, de), lambda b, i, e: (layer, e, 0, 0)),
                  pl.BlockSpec((None, None, d, de), lambda b, i, e: (layer, e, 0, 0)),
                  pl.BlockSpec((None, None, de, d), lambda b, i, e: (layer, e, 0, 0)),
                  xspec, _mod_spec(mod, layer, 5, tm, d),
                  pl.BlockSpec((1, d), lambda b, i, e: (0, 0))],
        out_specs=xspec,
        out_shape=jax.ShapeDtypeStruct((nb, r, d), F32),
        scratch_shapes=[pltpu.VMEM((tm, d), F32)],
        compiler_params=_cparams("parallel", "parallel", "arbitrary"),
        name="moe",
    )(h, gates, wg, wu, wd, x, mod, g_final)


def _kv_rows(k, v, heads, hd):
    return jnp.stack([k, v], axis=-2).reshape(k.shape[:-1] + (2, heads, hd))


def kernel(x_prompt, x_sample, cache_a0_kv, cache_a1_kv, cache_a2_kv, cache_b_kv, cache_c_kv, page_table,
           c_prompt, c_sample, rel_table, w_ada, b_ada, g_mix, g_ffn, g_final, w_in_a, w_out_a, w_in_b, sinks_b,
           w_out_b, w_in_c, lambda_c, subln_c, w_out_c, w_grp, b_grp, w_rt, b_rt, w_gate, w_up, w_down):
    depth, d = g_mix.shape
    bp, sp, _ = x_prompt.shape
    bs, ts, _ = x_sample.shape
    assert ts == 1
    a_caches_t = [c.transpose(0, 1, 3, 4, 5, 2) for c in (cache_a0_kv, cache_a1_kv, cache_a2_kv)]
    cache_b_t = cache_b_kv.transpose(0, 1, 3, 4, 5, 2)
    table = rel_table.astype(F32)

    m_all = _ada_all(jnp.concatenate([c_prompt, c_sample], axis=0), w_ada, b_ada)
    mod_p = m_all[:, :bp].reshape(depth, bp, 1, 6 * d)
    mod_s = m_all[:, bp:].reshape(depth, 1, bs, 6 * d)
    g_mix3 = g_mix.reshape(depth, 1, d)
    g_ffn3 = g_ffn.reshape(depth, 1, d)
    g_fin = g_final.reshape(1, d)

    w_in = [w_in_a.astype(BF16), w_in_b.astype(BF16), w_in_c.astype(BF16)]
    w_out = [w_out_a.astype(BF16), w_out_b.astype(BF16), w_out_c.astype(BF16)]
    wg, wu, wd = w_gate.astype(BF16), w_up.astype(BF16), w_down.astype(BF16)
    wr = jnp.concatenate([w_grp, w_rt.transpose(0, 2, 1, 3).reshape(depth, d, N_EXPERTS)], axis=-1)
    wr = jnp.pad(wr, ((0, 0), (0, 0), (0, ROUTER_LANES - wr.shape[-1])))
    br = jnp.concatenate([b_grp, b_rt.reshape(depth, N_EXPERTS)], axis=-1)
    br = jnp.pad(br, ((0, 0), (0, ROUTER_LANES - br.shape[-1]))).reshape(depth, 1, ROUTER_LANES)

    a_bias = [_band_bias(table, r, jnp.arange(A_HEADS)) for r in A_DILATIONS]
    wa = A_HEADS * A_HD

    xp = x_prompt
    xs = x_sample.reshape(1, bs, d)
    a_p = [[] for _ in range(A_GROUPS)]
    a_new = [None] * A_GROUPS
    b_p, b_s, c_p, c_s = [], [], [], []
    for i in range(depth):
        kind, li = i % N_MIXERS, i // N_MIXERS
        qkv_p = _norm_mod_matmul(xp, g_mix3, mod_p, i, w_in[kind][li])
        qkv_s = _norm_mod_matmul(xs, g_mix3, mod_s, i, w_in[kind][li])[0]
        if kind == 0:
            res_p = [_a_prompt_group(qkv_p, g, a_bias[g]) for g in range(A_GROUPS)]
            res_s = [_a_sample_group(qkv_s, a_caches_t[g], a_new[g], li, g, table) for g in range(A_GROUPS)]
            a_new = [r[2] for r in res_s]
            xp = _proj_res([r[0] for r in res_p], [r[1] for r in res_p], w_out[0][li], xp, mod_p, i)
            xs = _proj_res([r[0][None] for r in res_s], [r[1][None] for r in res_s], w_out[0][li], xs, mod_s, i)
            for g, L in enumerate(A_WINDOWS):
                lp = min(L, sp)
                kc, vc = (A_GROUPS + g) * wa, (2 * A_GROUPS + g) * wa
                a_p[g].append(_kv_rows(qkv_p[:, sp - lp:, kc:kc + wa], qkv_p[:, sp - lp:, vc:vc + wa], A_HEADS, A_HD))
        elif kind == 1:
            o_p = _b_prompt(qkv_p, sinks_b[li], table)
            o_s = _b_sample(qkv_s, cache_b_t, li, sinks_b[li], table)
            xp = _proj_res([o_p], [], w_out[1][li], xp, mod_p, i)
            xs = _proj_res([o_s[None]], [], w_out[1][li], xs, mod_s, i)
            nq, nk = B_HEADS * B_HD, B_KV_HEADS * B_HD
            lp = min(B_WINDOW, sp)
            b_p.append(_kv_rows(qkv_p[:, sp - lp:, nq:nq + nk], qkv_p[:, sp - lp:, nq + nk:], B_KV_HEADS, B_HD))
            b_s.append(_kv_rows(qkv_s[:, nq:nq + nk], qkv_s[:, nq + nk:], B_KV_HEADS, B_HD))
        else:
            lam_init = 0.8 - 0.6 * math.exp(-0.3 * i)
            o_p = _c_prompt(qkv_p, lambda_c[li], subln_c[li], table, lam_init)
            o_s = _c_sample(qkv_s, cache_c_kv, li, page_table, lambda_c[li], subln_c[li], table, lam_init)
            xp = _proj_res([o_p], [], w_out[2][li], xp, mod_p, i)
            xs = _proj_res([o_s[None]], [], w_out[2][li], xs, mod_s, i)
            nq, nk = C_HEADS * C_VD, C_KV_HEADS * C_VD
            c_p.append(_kv_rows(qkv_p[:, :, nq:nq + nk], qkv_p[:, :, nq + nk:], C_KV_HEADS, C_VD))
            c_s.append(_kv_rows(qkv_s[:, None, nq:nq + nk], qkv_s[:, None, nq + nk:], C_KV_HEADS, C_VD))
        last = i == depth - 1
        h_p, gates_p = _router(xp, g_ffn3, mod_p, i, wr, br)
        h_s, gates_s = _router(xs, g_ffn3, mod_s, i, wr, br)
        xp = _moe(h_p, gates_p, wg, wu, wd, xp, mod_p, i, g_fin, last)
        xs = _moe(h_s, gates_s, wg, wu, wd, xs, mod_s, i, g_fin, last)

    def shifted(cache, new_rows):
        return jnp.concatenate([cache[:, :, 1:], jnp.stack(new_rows)[:, :, None]], axis=2)

    a_out = [c.transpose(0, 1, 5, 2, 3, 4) for c in a_new]
    return (xp, xs.reshape(bs, ts, d),
            jnp.stack(a_p[0]), a_out[0],
            jnp.stack(a_p[1]), a_out[1],
            jnp.stack(a_p[2]), a_out[2],
            jnp.stack(b_p), shifted(cache_b_kv, b_s),
            jnp.stack(c_p), jnp.stack(c_s))
```

```python
import functools
import math

import jax
import jax.numpy as jnp
from jax import lax
from jax.experimental import pallas as pl
from jax.experimental.pallas import tpu as pltpu

F32 = jnp.float32
BF16 = jnp.bfloat16

N_MIXERS = 3
A_WINDOWS = (128, 512, 2048)
A_DILATIONS = (1, 4, 16)
A_GROUPS = 3
A_HEADS = 8
A_HD = 64
B_WINDOW = 128
B_HEADS = 16
B_KV_HEADS = 2
B_GROUP = B_HEADS // B_KV_HEADS
B_HD = 64
C_HEADS = 8
C_KV_HEADS = 4
C_GROUP = C_HEADS // C_KV_HEADS
C_HD = 64
C_VD = 2 * C_HD
NUM_BUCKETS = 32
MAX_DISTANCE = 2048
N_GROUPS = 4
EXPERTS_PER_GROUP = 8
N_EXPERTS = N_GROUPS * EXPERTS_PER_GROUP
PAGE_SIZE = 128
QBLOCK = 128
NORM_EPS = 1e-6
NEG = -1e30

LANES = 128
ROUTER_LANES = LANES
VMEM_LIMIT = 56 * 1024 * 1024


def _cparams(*sem):
    return pltpu.CompilerParams(dimension_semantics=sem, vmem_limit_bytes=VMEM_LIMIT)


def _largest_tile(n, cap, mult):
    best = None
    for t in range(mult, min(n, cap) + 1, mult):
        if n % t == 0:
            best = t
    assert best is not None, (n, cap, mult)
    return best


def _t5_bucket(dist):
    max_exact = NUM_BUCKETS // 2
    d = jnp.maximum(dist, 0)
    ratio = jnp.log(jnp.maximum(d, 1).astype(F32) / max_exact) / math.log(MAX_DISTANCE / max_exact)
    large = jnp.minimum(max_exact + (ratio * (NUM_BUCKETS - max_exact)).astype(jnp.int32), NUM_BUCKETS - 1)
    return jnp.where(d < max_exact, d, large)


def _rel_bias(table, dist):
    return table.astype(F32)[_t5_bucket(dist)]


def _toeplitz(w, rows, cols):
    m = w.shape[-1]
    assert m >= rows + cols - 1
    flat = jnp.tile(w, (1,) * (w.ndim - 1) + (rows,))[..., :rows * (m - 1)]
    return flat.reshape(w.shape[:-1] + (rows, m - 1))[..., :cols]


def _band_bias(table, dil, head_cols):
    m = 3 * QBLOCK
    u = jnp.arange(m, dtype=jnp.int32)
    u = jnp.where(u < 2 * QBLOCK, u, u - m)
    dist = QBLOCK - u
    ok = (dist >= 0) & (dist <= QBLOCK)
    w = jnp.where(ok[None], _rel_bias(table, dil * dist)[:, head_cols].T, NEG)
    return _toeplitz(w, QBLOCK, 2 * QBLOCK)


def _ada_kernel(c_ref, w_ref, b_ref, o_ref):
    c = c_ref[...]
    a = c * jax.nn.sigmoid(c)
    o_ref[...] = jnp.dot(a.astype(BF16), w_ref[...].astype(BF16), preferred_element_type=F32) + b_ref[...]


def _ada_all(c_all, w_ada, b_ada):
    depth, d, f = w_ada.shape
    n = c_all.shape[0]
    tn = _largest_tile(f, 1024, LANES)
    return pl.pallas_call(
        _ada_kernel,
        grid=(depth, f // tn),
        in_specs=[pl.BlockSpec((n, d), lambda i, j: (0, 0)),
                  pl.BlockSpec((None, d, tn), lambda i, j: (i, 0, j)),
                  pl.BlockSpec((None, 1, tn), lambda i, j: (i, 0, j))],
        out_specs=pl.BlockSpec((None, n, tn), lambda i, j: (i, 0, j)),
        out_shape=jax.ShapeDtypeStruct((depth, n, f), F32),
        compiler_params=_cparams("parallel", "parallel"),
        name="ada",
    )(c_all, w_ada, b_ada.reshape(depth, 1, f))


def _mod_spec(mod, layer, chunk, tm, width):
    per_row = mod.shape[2] != 1
    rows = tm if per_row else 1
    per = mod.shape[3] // 6 // width
    return pl.BlockSpec((None, None, rows, width),
                        lambda b, i, *_: (layer, b, i if per_row else 0, chunk * per))


def _norm_mod(x, g, shift, scale):
    y = x * lax.rsqrt(jnp.mean(x * x, axis=-1, keepdims=True) + NORM_EPS) * g
    return y * (1.0 + scale) + shift


def _nmm_kernel(x_ref, g_ref, sh_ref, sc_ref, w_ref, o_ref, h_ref):
    @pl.when(pl.program_id(2) == 0)
    def _():
        h_ref[...] = _norm_mod(x_ref[...], g_ref[...], sh_ref[...], sc_ref[...]).astype(h_ref.dtype)

    o_ref[...] = jnp.dot(h_ref[...], w_ref[...], preferred_element_type=F32)


def _norm_mod_matmul(x, g, mod, layer, w):
    nb, r, d = x.shape
    f = w.shape[1]
    tm = _largest_tile(r, 512, 8)
    tn = _largest_tile(f, 1536, LANES)
    return pl.pallas_call(
        _nmm_kernel,
        grid=(nb, r // tm, f // tn),
        in_specs=[pl.BlockSpec((None, tm, d), lambda b, i, j: (b, i, 0)),
                  pl.BlockSpec((None, 1, d), lambda b, i, j: (layer, 0, 0)),
                  _mod_spec(mod, layer, 0, tm, d),
                  _mod_spec(mod, layer, 1, tm, d),
                  pl.BlockSpec((d, tn), lambda b, i, j: (0, j))],
        out_specs=pl.BlockSpec((None, tm, tn), lambda b, i, j: (b, i, j)),
        out_shape=jax.ShapeDtypeStruct((nb, r, f), F32),
        scratch_shapes=[pltpu.VMEM((tm, d), BF16)],
        compiler_params=_cparams("parallel", "parallel", "arbitrary"),
        name="norm_qkv",
    )(x, g, mod, mod, w)


def _proj_res_kernel(o_ref, w_ref, x_ref, gate_ref, out_ref):
    y = jnp.dot(o_ref[...].astype(BF16), w_ref[...], preferred_element_type=F32)
    out_ref[...] = x_ref[...] + gate_ref[...] * y


def _merge_proj_res_kernel(o0, o1, o2, l0, l1, l2, w_ref, x_ref, gate_ref, out_ref):
    a0, a1, a2 = l0[...], l1[...], l2[...]
    m = jnp.maximum(jnp.maximum(a0, a1), a2)
    e0, e1, e2 = jnp.exp(a0 - m), jnp.exp(a1 - m), jnp.exp(a2 - m)
    o = (e0 * o0[...] + e1 * o1[...] + e2 * o2[...]) / (e0 + e1 + e2)
    y = jnp.dot(o.astype(BF16), w_ref[...], preferred_element_type=F32)
    out_ref[...] = x_ref[...] + gate_ref[...] * y


def _proj_res(os_, lses, w, x, mod, layer):
    nb, r, d = x.shape
    k = w.shape[0]
    tm = _largest_tile(r, 512, 8)
    ospec = pl.BlockSpec((None, tm, k), lambda b, i: (b, i, 0))
    xspec = pl.BlockSpec((None, tm, d), lambda b, i: (b, i, 0))
    ins = list(os_) + list(lses)
    kern = _merge_proj_res_kernel if lses else _proj_res_kernel
    return pl.pallas_call(
        kern,
        grid=(nb, r // tm),
        in_specs=[ospec] * len(ins) + [pl.BlockSpec((k, d), lambda b, i: (0, 0)), xspec,
                                       _mod_spec(mod, layer, 2, tm, d)],
        out_specs=xspec,
        out_shape=jax.ShapeDtypeStruct((nb, r, d), F32),
        compiler_params=_cparams("parallel", "parallel"),
        name="proj_res",
    )(*ins, w, x, mod)


def _a_prompt_kernel(q_ref, kp_ref, kc_ref, vp_ref, vc_ref, bias_ref, o_ref, l_ref):
    bi = pl.program_id(2)
    scale = A_HD ** -0.5
    q = q_ref[...].astype(BF16)
    k = jnp.concatenate([kp_ref[...], kc_ref[...]], axis=0).astype(BF16)
    v = jnp.concatenate([vp_ref[...], vc_ref[...]], axis=0).astype(BF16)
    col = lax.broadcasted_iota(jnp.int32, (QBLOCK, 2 * QBLOCK), 1)
    keep = (col >= QBLOCK) | (bi > 0)
    outs, lses = [], []
    for h in range(A_HEADS):
        sl = slice(h * A_HD, (h + 1) * A_HD)
        s = lax.dot_general(q[:, sl], k[:, sl], (((1,), (1,)), ((), ())), preferred_element_type=F32)
        s = jnp.where(keep, s * scale + bias_ref[h], NEG)
        m = jnp.max(s, axis=-1, keepdims=True)
        p = jnp.exp(s - m)
        l = jnp.sum(p, axis=-1, keepdims=True)
        outs.append(jnp.dot(p.astype(BF16), v[:, sl], preferred_element_type=F32) / l)
        lses.append(jnp.broadcast_to(m + jnp.log(l), (QBLOCK, A_HD)))
    o_ref[...] = jnp.concatenate(outs, axis=1)
    l_ref[...] = jnp.concatenate(lses, axis=1)


def _a_dilated_kernel(q_ref, kp_ref, kc_ref, vp_ref, vc_ref, bias_ref, o_ref, l_ref, *, dil):
    bi = pl.program_id(1)
    scale = A_HD ** -0.5
    col = lax.broadcasted_iota(jnp.int32, (QBLOCK, 2 * QBLOCK), 1)
    keep = (col >= QBLOCK) | (bi > 0)

    def one_class(c, carry):
        rows = pl.ds(c, QBLOCK, stride=dil)
        q = (q_ref[rows, :] * scale).astype(BF16)
        k = jnp.concatenate([kp_ref[rows, :], kc_ref[rows, :]], axis=0).astype(BF16)
        v = jnp.concatenate([vp_ref[rows, :], vc_ref[rows, :]], axis=0).astype(BF16)
        outs, lses = [], []
        for h in range(LANES // A_HD):
            sl = slice(h * A_HD, (h + 1) * A_HD)
            s = lax.dot_general(q[:, sl], k[:, sl], (((1,), (1,)), ((), ())), preferred_element_type=F32)
            s = jnp.where(keep, s + bias_ref[h], NEG)
            m = jnp.max(s, axis=-1, keepdims=True)
            p = jnp.exp(s - m)
            l = jnp.sum(p, axis=-1, keepdims=True)
            outs.append(jnp.dot(p.astype(BF16), v[:, sl], preferred_element_type=F32) / l)
            lses.append(jnp.broadcast_to(m + jnp.log(l), (QBLOCK, A_HD)))
        o_ref[rows, :] = jnp.concatenate(outs, axis=1)
        l_ref[rows, :] = jnp.concatenate(lses, axis=1)
        return carry

    lax.fori_loop(0, dil, one_class, 0, unroll=4)


def _a_dilated_group(qkv, g, bias):
    b, t, f = qkv.shape
    dil = A_DILATIONS[g]
    w = A_HEADS * A_HD
    chunk = QBLOCK * dil
    hpb = w // LANES
    hpl = LANES // A_HD

    def spec(sec, prev):
        return pl.BlockSpec((None, chunk, LANES),
                            lambda bb, i, hp: (bb, jnp.maximum(i - 1, 0) if prev else i,
                                               (sec * A_GROUPS + g) * hpb + hp))

    ospec = pl.BlockSpec((None, chunk, LANES), lambda bb, i, hp: (bb, i, hp))
    oshape = jax.ShapeDtypeStruct((b, t, w), F32)
    return pl.pallas_call(
        functools.partial(_a_dilated_kernel, dil=dil),
        grid=(b, t // chunk, hpb),
        in_specs=[spec(0, False), spec(1, True), spec(1, False), spec(2, True), spec(2, False),
                  pl.BlockSpec((hpl, QBLOCK, 2 * QBLOCK), lambda bb, i, hp: (hp, 0, 0))],
        out_specs=[ospec, ospec],
        out_shape=[oshape, oshape],
        compiler_params=_cparams("parallel", "parallel", "parallel"),
        name=f"a_prompt_g{g}",
    )(qkv, qkv, qkv, qkv, qkv, bias)


def _a_prompt_group(qkv, g, bias):
    if A_DILATIONS[g] > 1:
        return _a_dilated_group(qkv, g, bias)
    b, t, f = qkv.shape
    dil = A_DILATIONS[g]
    lr = t // dil
    nq = lr // QBLOCK
    w = A_HEADS * A_HD
    per = f // w
    view = qkv.reshape(b, lr, dil * f)

    def spec(sec, prev):
        return pl.BlockSpec((None, QBLOCK, w),
                            lambda bb, c, i: (bb, jnp.maximum(i - 1, 0) if prev else i, c * per + sec * A_GROUPS + g))

    ospec = pl.BlockSpec((None, QBLOCK, w), lambda bb, c, i: (bb, i, c))
    oshape = jax.ShapeDtypeStruct((b, lr, dil * w), F32)
    o, l = pl.pallas_call(
        _a_prompt_kernel,
        grid=(b, dil, nq),
        in_specs=[spec(0, False), spec(1, True), spec(1, False), spec(2, True), spec(2, False),
                  pl.BlockSpec((A_HEADS, QBLOCK, 2 * QBLOCK), lambda bb, c, i: (0, 0, 0))],
        out_specs=[ospec, ospec],
        out_shape=[oshape, oshape],
        compiler_params=_cparams("parallel", "parallel", "parallel"),
        name=f"a_prompt_g{g}",
    )(view, view, view, view, view, bias)
    return o.reshape(b, t, w), l.reshape(b, t, w)


def _decode_head(kt, vt, q, kn, vn, bias, bias_new, scale, sink=None):
    s = jnp.sum(kt * q, axis=1, keepdims=True) * scale + bias
    s_new = jnp.sum(kn * q, axis=1, keepdims=True) * scale + bias_new
    m = jnp.maximum(jnp.max(s, axis=-1, keepdims=True), s_new)
    if sink is not None:
        m = jnp.maximum(m, sink)
    p = jnp.exp(s - m)
    pn = jnp.exp(s_new - m)
    l = jnp.sum(p, axis=-1, keepdims=True) + pn
    if sink is not None:
        l = l + jnp.exp(sink - m)
    o = (jnp.sum(vt * p, axis=-1, keepdims=True) + pn * vn) / l
    return o, m + jnp.log(l)


def _a_sample_kernel(q_ref, kvn_ref, c_ref, bias_ref, bn_ref, *rest):
    o_ref, l_ref, out_ref = rest[-3:]
    L = c_ref.shape[-1]
    scale = A_HD ** -0.5
    last = lax.broadcasted_iota(jnp.int32, (1, 1, L), 2) == L - 1
    for h in range(A_HEADS):
        kt, vt = c_ref[:, 0, h], c_ref[:, 1, h]
        kn, vn = kvn_ref[:, 0, h], kvn_ref[:, 1, h]
        o, lse = _decode_head(kt, vt, q_ref[:, h], kn, vn, bias_ref[h], bn_ref[h], scale)
        o_ref[:, h] = o
        l_ref[:, h] = lse
        out_ref[:, 0, h] = jnp.where(last, kn, pltpu.roll(kt, L - 1, 2))
        out_ref[:, 1, h] = jnp.where(last, vn, pltpu.roll(vt, L - 1, 2))


def _a_sample_group(qkv, cache_t, prev_out, li, g, table):
    n, f = qkv.shape
    dil = A_DILATIONS[g]
    n_lay, nb, two, hh, hd, L = cache_t.shape
    assert (nb, two, hh, hd) == (n, 2, A_HEADS, A_HD) and L == A_WINDOWS[g]
    w = A_HEADS * A_HD
    col = lambda sec: qkv[:, (sec * A_GROUPS + g) * w:(sec * A_GROUPS + g + 1) * w]
    q = col(0).reshape(n, A_HEADS, A_HD, 1)
    kvn = jnp.stack([col(1), col(2)], axis=1).reshape(n, 2, A_HEADS, A_HD, 1)
    dist = L - jnp.arange(L, dtype=jnp.int32)
    bias = jnp.where((dist % dil == 0)[None], _rel_bias(table, dist).T, NEG).reshape(A_HEADS, 1, L)
    bn = _rel_bias(table, jnp.zeros((1,), jnp.int32)).T.reshape(A_HEADS, 1, 1)
    bt = max(1, (4 * 1024 * 1024) // (2 * w * L * 4))
    cspec = pl.BlockSpec((None, bt, 2, A_HEADS, A_HD, L), lambda b: (li, b, 0, 0, 0, 0))
    in_specs = [pl.BlockSpec((bt, A_HEADS, A_HD, 1), lambda b: (b, 0, 0, 0)),
                pl.BlockSpec((bt, 2, A_HEADS, A_HD, 1), lambda b: (b, 0, 0, 0, 0)),
                cspec,
                pl.BlockSpec((A_HEADS, 1, L), lambda b: (0, 0, 0)),
                pl.BlockSpec((A_HEADS, 1, 1), lambda b: (0, 0, 0))]
    args = [q, kvn, cache_t, bias, bn]
    aliases = {}
    if prev_out is not None:
        in_specs.append(pl.BlockSpec(memory_space=pl.ANY))
        args.append(prev_out)
        aliases = {len(args) - 1: 2}
    o, lse, out = pl.pallas_call(
        _a_sample_kernel,
        grid=(n // bt,),
        in_specs=in_specs,
        out_specs=[pl.BlockSpec((bt, A_HEADS, A_HD, 1), lambda b: (b, 0, 0, 0)),
                   pl.BlockSpec((bt, A_HEADS, 1, 1), lambda b: (b, 0, 0, 0)),
                   cspec],
        out_shape=[jax.ShapeDtypeStruct((n, A_HEADS, A_HD, 1), F32),
                   jax.ShapeDtypeStruct((n, A_HEADS, 1, 1), F32),
                   jax.ShapeDtypeStruct(cache_t.shape, F32)],
        input_output_aliases=aliases,
        compiler_params=_cparams("parallel"),
        name=f"a_sample_g{g}",
    )(*args)
    lse = jnp.broadcast_to(lse.reshape(n, A_HEADS, 1), (n, A_HEADS, A_HD)).reshape(n, w)
    return o.reshape(n, w), lse, out


def _head_mask(rows, width, hd, lane_head_of_row):
    lane = lax.broadcasted_iota(jnp.int32, (rows, width), 1) // hd
    row = lax.broadcasted_iota(jnp.int32, (rows, width), 0)
    return (lane == lane_head_of_row(row)).astype(F32)


def _b_prompt_kernel(q_ref, kp_ref, kc_ref, vp_ref, vc_ref, bias_ref, sink_ref, o_ref):
    bi = pl.program_id(1)
    scale = B_HD ** -0.5
    q = q_ref[...].astype(BF16)
    k = jnp.concatenate([kp_ref[...], kc_ref[...]], axis=0).astype(BF16)
    v = jnp.concatenate([vp_ref[...], vc_ref[...]], axis=0).astype(BF16)
    col = lax.broadcasted_iota(jnp.int32, (B_GROUP * QBLOCK, 2 * QBLOCK), 1)
    keep = (col >= QBLOCK) | (bi > 0)
    outs = []
    for kh in range(B_KV_HEADS):
        qs = jnp.concatenate([q[:, (kh * B_GROUP + g) * B_HD:(kh * B_GROUP + g + 1) * B_HD]
                              for g in range(B_GROUP)], axis=0)
        sl = slice(kh * B_HD, (kh + 1) * B_HD)
        s = lax.dot_general(qs, k[:, sl], (((1,), (1,)), ((), ())), preferred_element_type=F32)
        s = jnp.where(keep, s * scale + bias_ref[kh], NEG)
        sink = sink_ref[kh]
        m = jnp.maximum(jnp.max(s, axis=-1, keepdims=True), sink)
        p = jnp.exp(s - m)
        den = jnp.sum(p, axis=-1, keepdims=True) + jnp.exp(sink - m)
        o = jnp.dot(p.astype(BF16), v[:, sl], preferred_element_type=F32) / den
        outs += [o[g * QBLOCK:(g + 1) * QBLOCK] for g in range(B_GROUP)]
    o_ref[...] = jnp.concatenate(outs, axis=1)


def _b_prompt(qkv, sinks, table):
    b, t, f = qkv.shape
    nq = t // QBLOCK
    wq = B_HEADS * B_HD
    wk = B_KV_HEADS * B_HD
    head_cols = jnp.arange(B_HEADS) // (B_HEADS // table.shape[1])
    bias = _band_bias(table, 1, head_cols).reshape(B_KV_HEADS, B_GROUP * QBLOCK, 2 * QBLOCK)
    sink = jnp.broadcast_to(sinks.astype(F32).reshape(B_KV_HEADS, B_GROUP, 1, 1),
                            (B_KV_HEADS, B_GROUP, QBLOCK, 1)).reshape(B_KV_HEADS, B_GROUP * QBLOCK, 1)

    def kvspec(colblock, prev):
        return pl.BlockSpec((None, QBLOCK, wk), lambda bb, i: (bb, jnp.maximum(i - 1, 0) if prev else i, colblock))

    kcol = wq // wk
    return pl.pallas_call(
        _b_prompt_kernel,
        grid=(b, nq),
        in_specs=[pl.BlockSpec((None, QBLOCK, wq), lambda bb, i: (bb, i, 0)),
                  kvspec(kcol, True), kvspec(kcol, False), kvspec(kcol + 1, True), kvspec(kcol + 1, False),
                  pl.BlockSpec(bias.shape, lambda bb, i: (0, 0, 0)),
                  pl.BlockSpec(sink.shape, lambda bb, i: (0, 0, 0))],
        out_specs=pl.BlockSpec((None, QBLOCK, wq), lambda bb, i: (bb, i, 0)),
        out_shape=jax.ShapeDtypeStruct((b, t, wq), F32),
        compiler_params=_cparams("parallel", "parallel"),
        name="b_prompt",
    )(qkv, qkv, qkv, qkv, qkv, bias, sink)


def _b_sample_kernel(q_ref, kvn_ref, c_ref, bias_ref, bn_ref, sink_ref, o_ref):
    scale = B_HD ** -0.5
    for hq in range(B_HEADS):
        kh = hq // B_GROUP
        o, _ = _decode_head(c_ref[:, 0, kh], c_ref[:, 1, kh], q_ref[:, hq], kvn_ref[:, 0, kh], kvn_ref[:, 1, kh],
                            bias_ref[hq], bn_ref[hq], scale, sink_ref[hq])
        o_ref[:, hq] = o


def _b_sample(qkv, cache_t, li, sinks, table):
    n, f = qkv.shape
    n_lay, nb, two, kvh, hd, L = cache_t.shape
    assert (nb, two, kvh, hd) == (n, 2, B_KV_HEADS, B_HD) and L == B_WINDOW
    wq = B_HEADS * B_HD
    wk = B_KV_HEADS * B_HD
    head_cols = jnp.arange(B_HEADS) // (B_HEADS // table.shape[1])
    dist = L - jnp.arange(L, dtype=jnp.int32)
    bias = _rel_bias(table, dist)[:, head_cols].T.reshape(B_HEADS, 1, L)
    bn = _rel_bias(table, jnp.zeros((1,), jnp.int32))[:, head_cols].T.reshape(B_HEADS, 1, 1)
    q = qkv[:, :wq].reshape(n, B_HEADS, B_HD, 1)
    kvn = jnp.stack([qkv[:, wq:wq + wk], qkv[:, wq + wk:]], axis=1).reshape(n, 2, B_KV_HEADS, B_HD, 1)
    bt = 8
    o = pl.pallas_call(
        _b_sample_kernel,
        grid=(n // bt,),
        in_specs=[pl.BlockSpec((bt, B_HEADS, B_HD, 1), lambda b: (b, 0, 0, 0)),
                  pl.BlockSpec((bt, 2, B_KV_HEADS, B_HD, 1), lambda b: (b, 0, 0, 0, 0)),
                  pl.BlockSpec((None, bt, 2, B_KV_HEADS, B_HD, L), lambda b: (li, b, 0, 0, 0, 0)),
                  pl.BlockSpec((B_HEADS, 1, L), lambda b: (0, 0, 0)),
                  pl.BlockSpec((B_HEADS, 1, 1), lambda b: (0, 0, 0)),
                  pl.BlockSpec((B_HEADS, 1, 1), lambda b: (0, 0, 0))],
        out_specs=pl.BlockSpec((bt, B_HEADS, B_HD, 1), lambda b: (b, 0, 0, 0)),
        out_shape=jax.ShapeDtypeStruct((n, B_HEADS, B_HD, 1), F32),
        compiler_params=_cparams("parallel"),
        name="b_sample",
    )(q, kvn, cache_t, bias, bn, sinks.astype(F32).reshape(B_HEADS, 1, 1))
    return o.reshape(n, wq)


def _diff_lambda(lp, lam_init):
    return (jnp.exp(jnp.sum(lp[0:1] * lp[1:2], axis=-1, keepdims=True))
            - jnp.exp(jnp.sum(lp[2:3] * lp[3:4], axis=-1, keepdims=True)) + lam_init)


def _diff_out(o1, o2, lam, subln, lam_init):
    d = o1 - lam * o2
    y = d * lax.rsqrt(jnp.mean(d * d, axis=-1, keepdims=True) + NORM_EPS) * subln
    return y * (1.0 - lam_init)


C_BLOCK = 512


def _c_prompt_kernel(qi_tab, ki_tab, q_ref, k_ref, v_ref, bias_ref, lam_ref, subln_ref, o_ref,
                     m_sc, l_sc, acc_sc, *, lam_init):
    t = pl.program_id(2)
    qi = qi_tab[t]
    ki = ki_tab[t]
    scale = C_HD ** -0.5

    @pl.when(ki == 0)
    def _():
        m_sc[...] = jnp.full(m_sc.shape, NEG, F32)
        l_sc[...] = jnp.zeros(l_sc.shape, F32)
        acc_sc[...] = jnp.zeros(acc_sc.shape, F32)

    q = (q_ref[...] * scale).astype(BF16)
    k = k_ref[...].astype(BF16)
    v = v_ref[...].astype(BF16)
    for g in range(C_GROUP):
        bias = bias_ref[g]
        for mp in range(2):
            idx = g * 2 + mp
            qs = q[:, g * C_VD + mp * C_HD:g * C_VD + (mp + 1) * C_HD]
            ks = k[:, mp * C_HD:(mp + 1) * C_HD]
            s = lax.dot_general(qs, ks, (((1,), (1,)), ((), ())), preferred_element_type=F32) + bias
            m_old = m_sc[idx]
            m_new = jnp.maximum(m_old, jnp.max(s, axis=-1, keepdims=True))
            alpha = jnp.exp(m_old - m_new)
            p = jnp.exp(s - m_new)
            l_sc[idx] = alpha * l_sc[idx] + jnp.sum(p, axis=-1, keepdims=True)
            acc_sc[idx] = alpha * acc_sc[idx] + jnp.dot(p.astype(BF16), v, preferred_element_type=F32)
            m_sc[idx] = m_new

    @pl.when(ki == qi)
    def _():
        lam = _diff_lambda(lam_ref[...], lam_init)
        outs = []
        for g in range(C_GROUP):
            o1 = acc_sc[g * 2] / l_sc[g * 2]
            o2 = acc_sc[g * 2 + 1] / l_sc[g * 2 + 1]
            outs.append(_diff_out(o1, o2, lam, subln_ref[...], lam_init))
        o_ref[...] = jnp.concatenate(outs, axis=1)


def _c_prompt(qkv, lam_p, subln, table, lam_init):
    b, t, f = qkv.shape
    blk = C_BLOCK
    nblk = t // blk
    pairs = [(qi, ki) for qi in range(nblk) for ki in range(qi + 1)]
    qi_tab = jnp.asarray([p[0] for p in pairs], jnp.int32)
    ki_tab = jnp.asarray([p[1] for p in pairs], jnp.int32)
    u = jnp.arange(2 * blk, dtype=jnp.int32)
    u = jnp.where(u < blk, u, u - 2 * blk)
    dist = jnp.arange(nblk, dtype=jnp.int32)[:, None] * blk - u[None, :]
    w = jnp.where((dist >= 0)[:, None], _rel_bias(table, dist).transpose(0, 2, 1), NEG)
    bias = _toeplitz(w, blk, blk)
    wq = C_GROUP * C_VD
    qcols = C_HEADS * C_VD
    grid_spec = pltpu.PrefetchScalarGridSpec(
        num_scalar_prefetch=2,
        grid=(b, C_KV_HEADS, len(pairs)),
        in_specs=[pl.BlockSpec((None, blk, wq), lambda bb, kh, tt, qt, kt: (bb, qt[tt], kh)),
                  pl.BlockSpec((None, blk, C_VD), lambda bb, kh, tt, qt, kt: (bb, kt[tt], qcols // C_VD + kh)),
                  pl.BlockSpec((None, blk, C_VD),
                               lambda bb, kh, tt, qt, kt: (bb, kt[tt], qcols // C_VD + C_KV_HEADS + kh)),
                  pl.BlockSpec((None, C_GROUP, blk, blk), lambda bb, kh, tt, qt, kt: (qt[tt] - kt[tt], kh, 0, 0)),
                  pl.BlockSpec((4, C_HD), lambda bb, kh, tt, qt, kt: (0, 0)),
                  pl.BlockSpec((1, C_VD), lambda bb, kh, tt, qt, kt: (0, 0))],
        out_specs=pl.BlockSpec((None, blk, wq), lambda bb, kh, tt, qt, kt: (bb, qt[tt], kh)),
        scratch_shapes=[pltpu.VMEM((2 * C_GROUP, blk, 1), F32), pltpu.VMEM((2 * C_GROUP, blk, 1), F32),
                        pltpu.VMEM((2 * C_GROUP, blk, C_VD), F32)],
    )
    return pl.pallas_call(
        functools.partial(_c_prompt_kernel, lam_init=lam_init),
        grid_spec=grid_spec,
        out_shape=jax.ShapeDtypeStruct((b, t, qcols), F32),
        compiler_params=_cparams("parallel", "parallel", "arbitrary"),
        name="c_prompt",
    )(qi_tab, ki_tab, qkv, qkv, qkv, bias, lam_p.astype(F32), subln.astype(F32).reshape(1, C_VD))


C_PAGES_PER_STEP = 8
C_MAPS = 2 * C_HEADS
C_ROWS = 2 * C_KV_HEADS


def _c_sample_kernel(pt_ref, q_ref, kvn_ref, lam_ref, subln_ref, bias_ref, bn_ref, *rest, lam_init):
    pages = rest[:C_PAGES_PER_STEP]
    o_ref, m_sc, l_sc, acc_sc = rest[C_PAGES_PER_STEP:]
    j = pl.program_id(1)
    scale = C_HD ** -0.5
    hm = _head_mask(C_MAPS, C_VD, C_HD, lambda r: r // C_HEADS)
    qe = jnp.concatenate([q_ref[...]] * 2, axis=-1) * hm
    qb = (qe * scale).astype(BF16)
    row_kh = (lax.broadcasted_iota(jnp.int32, (C_MAPS, 1), 0) % C_HEADS) // C_GROUP

    @pl.when(j == 0)
    def _():
        kvn = kvn_ref[...]
        s_new = jnp.zeros((C_MAPS, 1), F32)
        v_new = jnp.zeros((C_MAPS, C_VD), F32)
        for kh in range(C_KV_HEADS):
            s_kh = jnp.sum(qe * kvn[kh:kh + 1], axis=-1, keepdims=True)
            s_new = jnp.where(row_kh == kh, s_kh, s_new)
            v_new = jnp.where(row_kh == kh, kvn[C_KV_HEADS + kh:C_KV_HEADS + kh + 1], v_new)
        m_sc[...] = s_new * scale + bn_ref[...]
        l_sc[...] = jnp.ones(l_sc.shape, F32)
        acc_sc[...] = v_new

    rows = [pages[pi][...].astype(BF16) for pi in range(C_PAGES_PER_STEP)]
    ss = [lax.dot_general(qb, rows[pi], (((1,), (1,)), ((), ())), preferred_element_type=F32) + bias_ref[pi]
          for pi in range(C_PAGES_PER_STEP)]
    m = m_sc[...]
    m_new = m
    for s in ss:
        m_new = jnp.maximum(m_new, jnp.max(s, axis=-1, keepdims=True))
    alpha = jnp.exp(m - m_new)
    l = alpha * l_sc[...]
    acc = alpha * acc_sc[...]
    for pi in range(C_PAGES_PER_STEP):
        p = jnp.exp(ss[pi] - m_new)
        l = l + jnp.sum(p, axis=-1, keepdims=True)
        pv = pltpu.roll(p, C_KV_HEADS, 1)
        acc = acc + jnp.dot(pv.astype(BF16), rows[pi], preferred_element_type=F32)
    m_sc[...] = m_new
    l_sc[...] = l
    acc_sc[...] = acc

    @pl.when(j == pl.num_programs(1) - 1)
    def _():
        o = acc / l
        lam = _diff_lambda(lam_ref[...], lam_init)
        o_ref[...] = _diff_out(o[:C_HEADS], o[C_HEADS:], lam, subln_ref[...], lam_init)


def _c_sample(qkv, cache, ci, page_table, lam_p, subln, table, lam_init):
    n, f = qkv.shape
    n_lay, n_phys, page, two, kvh, vd = cache.shape
    assert (page, two, kvh, vd) == (PAGE_SIZE, 2, C_KV_HEADS, C_VD)
    n_pages = page_table.shape[1]
    pps = C_PAGES_PER_STEP
    assert n_pages % pps == 0
    qcols = C_HEADS * C_VD
    prow = page * C_ROWS
    view = cache.reshape(n_lay, n_phys, prow, C_VD)
    q16 = qkv[:, :qcols].reshape(n, C_HEADS, 2, C_HD).transpose(0, 2, 1, 3).reshape(n, C_MAPS, C_HD)
    kvn = qkv[:, qcols:].reshape(n, C_ROWS, C_VD)
    n_keys = n_pages * page
    b8 = _rel_bias(table, n_keys - jnp.arange(n_keys, dtype=jnp.int32)).T
    b16 = jnp.concatenate([b8, b8], axis=0)
    row_kh = (jnp.arange(C_MAPS) % C_HEADS) // C_GROUP
    own = row_kh[:, None, None] == jnp.arange(C_ROWS)[None, None, :]
    bias = jnp.where(own, b16[:, :, None], NEG).reshape(C_MAPS, n_pages, prow).transpose(1, 0, 2)
    bn = _rel_bias(table, jnp.zeros((1,), jnp.int32)).T
    bn = jnp.concatenate([bn, bn], axis=0)
    pt_flat = page_table.reshape(-1).astype(jnp.int32)

    def page_spec(pi):
        return pl.BlockSpec((None, None, prow, C_VD),
                            lambda b, j, pt: (ci, pt[b * n_pages + j * pps + pi], 0, 0))

    grid_spec = pltpu.PrefetchScalarGridSpec(
        num_scalar_prefetch=1,
        grid=(n, n_pages // pps),
        in_specs=[pl.BlockSpec((None, C_MAPS, C_HD), lambda b, j, pt: (b, 0, 0)),
                  pl.BlockSpec((None, C_ROWS, C_VD), lambda b, j, pt: (b, 0, 0)),
                  pl.BlockSpec((4, C_HD), lambda b, j, pt: (0, 0)),
                  pl.BlockSpec((1, C_VD), lambda b, j, pt: (0, 0)),
                  pl.BlockSpec((pps, C_MAPS, prow), lambda b, j, pt: (j, 0, 0)),
                  pl.BlockSpec((C_MAPS, 1), lambda b, j, pt: (0, 0))] + [page_spec(pi) for pi in range(pps)],
        out_specs=pl.BlockSpec((None, C_HEADS, C_VD), lambda b, j, pt: (b, 0, 0)),
        scratch_shapes=[pltpu.VMEM((C_MAPS, 1), F32), pltpu.VMEM((C_MAPS, 1), F32), pltpu.VMEM((C_MAPS, C_VD), F32)],
    )
    o = pl.pallas_call(
        functools.partial(_c_sample_kernel, lam_init=lam_init),
        grid_spec=grid_spec,
        out_shape=jax.ShapeDtypeStruct((n, C_HEADS, C_VD), F32),
        compiler_params=_cparams("parallel", "arbitrary"),
        name="c_sample",
    )(pt_flat, q16, kvn, lam_p.astype(F32), subln.astype(F32).reshape(1, C_VD), bias, bn, *([view] * pps))
    return o.reshape(n, qcols)


def _route(x, g, shift, scale, wr, br, h_ref, sel_ref):
    h = _norm_mod(x, g, shift, scale)
    h_ref[...] = h
    logits = jnp.dot(h, wr, preferred_element_type=F32, precision=lax.Precision.HIGHEST) + br
    lane = lax.broadcasted_iota(jnp.int32, logits.shape, 1).astype(F32)
    first = lambda hit: jnp.min(jnp.where(hit, lane, float(ROUTER_LANES)), axis=-1, keepdims=True)
    is_g = lane < N_GROUPS
    gl = jnp.where(is_g, logits, NEG)
    gmax = jnp.max(gl, axis=-1, keepdims=True)
    g_sel = first(is_g & (gl == gmax))
    g_prob = 1.0 / jnp.sum(jnp.where(is_g, jnp.exp(gl - gmax), 0.0), axis=-1, keepdims=True)
    lo = N_GROUPS + EXPERTS_PER_GROUP * g_sel
    in_grp = (lane >= lo) & (lane < lo + EXPERTS_PER_GROUP)
    el = jnp.where(in_grp, logits, NEG)
    v1 = jnp.max(el, axis=-1, keepdims=True)
    i1 = first(in_grp & (el == v1))
    others = in_grp & (lane != i1)
    el2 = jnp.where(others, el, NEG)
    v2 = jnp.max(el2, axis=-1, keepdims=True)
    i2 = first(others & (el2 == v2))
    e2 = jnp.exp(v2 - v1)
    w1 = g_prob / (1.0 + e2)
    w2 = g_prob * e2 / (1.0 + e2)
    sel = jnp.where(lane == 0.0, i1 - N_GROUPS, jnp.where(lane == 1.0, i2 - N_GROUPS, 0.0))
    sel_ref[...] = jnp.where(lane == 2.0, w1, jnp.where(lane == 3.0, w2, sel))


def _router_kernel(xp_ref, xs_ref, g_ref, shp_ref, scp_ref, shs_ref, scs_ref, wr_ref, br_ref, h_ref, sel_ref,
                   *, prompt_tiles):
    i = pl.program_id(0)

    @pl.when(i < prompt_tiles)
    def _():
        _route(xp_ref[...], g_ref[...], shp_ref[...], scp_ref[...], wr_ref[...], br_ref[...], h_ref, sel_ref)

    @pl.when(i >= prompt_tiles)
    def _():
        _route(xs_ref[...], g_ref[...], shs_ref[...], scs_ref[...], wr_ref[...], br_ref[...], h_ref, sel_ref)


def _router(xp, xs, g, mod_p, mod_s, layer, wr, br):
    bp, sp, d = xp.shape
    _, bs, _ = xs.shape
    tm = math.gcd(math.gcd(sp, bs), LANES)
    tpb = sp // tm
    pt = bp * tpb
    st = bs // tm
    n_all = bp * sp + bs
    pidx = lambda i: jnp.minimum(i, pt - 1)
    sidx = lambda i: jnp.maximum(i - pt, 0)
    mp = lambda chunk: pl.BlockSpec((None, None, 1, d), lambda i: (layer, pidx(i) // tpb, 0, chunk))
    ms = lambda chunk: pl.BlockSpec((None, None, tm, d), lambda i: (layer, 0, sidx(i), chunk))
    return pl.pallas_call(
        functools.partial(_router_kernel, prompt_tiles=pt),
        grid=(pt + st,),
        in_specs=[pl.BlockSpec((None, tm, d), lambda i: (pidx(i) // tpb, pidx(i) % tpb, 0)),
                  pl.BlockSpec((None, tm, d), lambda i: (0, sidx(i), 0)),
                  pl.BlockSpec((None, 1, d), lambda i: (layer, 0, 0)),
                  mp(3), mp(4), ms(3), ms(4),
                  pl.BlockSpec((None, d, ROUTER_LANES), lambda i: (layer, 0, 0)),
                  pl.BlockSpec((None, 1, ROUTER_LANES), lambda i: (layer, 0, 0))],
        out_specs=[pl.BlockSpec((tm, d), lambda i: (i, 0)),
                   pl.BlockSpec((tm, ROUTER_LANES), lambda i: (i, 0))],
        out_shape=[jax.ShapeDtypeStruct((n_all, d), F32), jax.ShapeDtypeStruct((n_all, ROUTER_LANES), F32)],
        compiler_params=_cparams("parallel"),
        name="router",
    )(xp, xs, g, mod_p, mod_p, mod_s, mod_s, wr, br)


MOE_TM = 256
TOP_K = 2


def _route_slots(sel, tm):
    n = sel.shape[0]
    n_pairs = TOP_K * n
    n_tiles = (n_pairs + N_EXPERTS * (tm - 1)) // tm + 1
    p_max = n_tiles * tm
    eid = sel[:, :TOP_K].astype(jnp.int32).T.reshape(-1)
    wts = sel[:, TOP_K:2 * TOP_K].T.reshape(-1)
    onehot = (eid[:, None] == jnp.arange(N_EXPERTS, dtype=jnp.int32)[None, :]).astype(jnp.int32)
    cums = jnp.cumsum(onehot, axis=0)
    rank = jnp.sum(cums * onehot, axis=1) - 1
    counts = cums[-1]
    padded = ((counts + tm - 1) // tm) * tm
    ends = jnp.cumsum(padded)
    pos = (ends - padded)[eid] + rank
    pair = jnp.arange(n_pairs, dtype=jnp.int32)
    slot = jnp.arange(p_max, dtype=jnp.int32)
    src = jnp.zeros((p_max,), jnp.int32).at[pos].set(pair % n)
    dst = (n_pairs + slot % (2 * tm)).at[pos].set(pair)
    w_slot = jnp.zeros((p_max,), F32).at[pos].set(wts).reshape(p_max, 1)
    tile_start = jnp.arange(n_tiles, dtype=jnp.int32) * tm
    tile_expert = jnp.minimum(jnp.sum((ends[None, :] <= tile_start[:, None]).astype(jnp.int32), axis=1),
                              N_EXPERTS - 1)
    n_used = (ends[-1] // tm).reshape(1)
    return tile_expert, src, dst, n_used, w_slot


def _experts_kernel(te_ref, src_ref, dst_ref, nu_ref, h_hbm, ws_ref, wg_ref, wu_ref, wd_ref, y_hbm,
                    xbuf, obuf, wgb, wub, wdb, gsem, ssem):
    t = pl.program_id(0)
    n_used = nu_ref[0]
    slot = t % 2
    tm = xbuf.shape[1]

    def gather(tile, s):
        def body(r, c):
            pltpu.make_async_copy(h_hbm.at[pl.ds(src_ref[tile * tm + r], 1)], xbuf.at[s, pl.ds(r, 1)],
                                  gsem.at[s]).start()
            return c
        lax.fori_loop(0, tm, body, 0)

    def scatter(tile, s):
        def body(r, c):
            pltpu.make_async_copy(obuf.at[s, pl.ds(r, 1)], y_hbm.at[pl.ds(dst_ref[tile * tm + r], 1)],
                                  ssem.at[s]).start()
            return c
        lax.fori_loop(0, tm, body, 0)

    def wait_gather(s):
        pltpu.make_async_copy(h_hbm.at[pl.ds(0, tm)], xbuf.at[s], gsem.at[s]).wait()

    def wait_scatter(s):
        pltpu.make_async_copy(obuf.at[s], y_hbm.at[pl.ds(0, tm)], ssem.at[s]).wait()

    @pl.when(t == 0)
    def _():
        obuf[0] = jnp.zeros(obuf.shape[1:], F32)
        n_pairs = y_hbm.shape[0] - 2 * tm
        for half in range(2):
            fill = pltpu.make_async_copy(obuf.at[0], y_hbm.at[pl.ds(n_pairs + half * tm, tm)], ssem.at[0])
            fill.start()
            fill.wait()
        gather(0, 0)

    @pl.when(t < n_used)
    def _():
        wait_gather(slot)

        @pl.when(t + 1 < n_used)
        def _():
            gather(t + 1, 1 - slot)

        @pl.when((t == 0) | (te_ref[t] != te_ref[jnp.maximum(t - 1, 0)]))
        def _():
            wgb[...] = wg_ref[...].astype(BF16)
            wub[...] = wu_ref[...].astype(BF16)
            wdb[...] = wd_ref[...].astype(BF16)

        @pl.when(t >= 2)
        def _():
            wait_scatter(slot)

        x = xbuf[slot].astype(BF16)
        a = jnp.dot(x, wgb[...], preferred_element_type=F32)
        u = jnp.dot(x, wub[...], preferred_element_type=F32)
        hid = (a * jax.nn.sigmoid(a)) * u * ws_ref[...]
        obuf[slot] = jnp.dot(hid.astype(BF16), wdb[...], preferred_element_type=F32)
        scatter(t, slot)

        @pl.when(t == n_used - 1)
        def _():
            @pl.when(t >= 1)
            def _():
                wait_scatter(1 - slot)
            wait_scatter(slot)


def _experts(h, sel, wg, wu, wd, layer):
    n, d = h.shape
    de = wg.shape[3]
    tm = MOE_TM
    tile_expert, src, dst, n_used, w_slot = _route_slots(sel, tm)
    n_tiles = tile_expert.shape[0]
    wspec = lambda a, b: pl.BlockSpec((None, None, a, b), lambda t, te, s_, d_, nu: (layer, te[t], 0, 0))
    grid_spec = pltpu.PrefetchScalarGridSpec(
        num_scalar_prefetch=4,
        grid=(n_tiles,),
        in_specs=[pl.BlockSpec(memory_space=pl.ANY),
                  pl.BlockSpec((tm, 1), lambda t, te, s_, d_, nu: (t, 0)),
                  wspec(d, de), wspec(d, de), wspec(de, d)],
        out_specs=pl.BlockSpec(memory_space=pl.ANY),
        scratch_shapes=[pltpu.VMEM((2, tm, d), F32), pltpu.VMEM((2, tm, d), F32),
                        pltpu.VMEM((d, de), BF16), pltpu.VMEM((d, de), BF16), pltpu.VMEM((de, d), BF16),
                        pltpu.SemaphoreType.DMA((2,)), pltpu.SemaphoreType.DMA((2,))],
    )
    return pl.pallas_call(
        _experts_kernel,
        grid_spec=grid_spec,
        out_shape=jax.ShapeDtypeStruct((TOP_K * n + 2 * tm, d), F32),
        compiler_params=_cparams("arbitrary"),
        name="experts",
    )(tile_expert, src, dst, n_used, h, w_slot, wg, wu, wd)


def _combine_kernel(x_ref, y0_ref, y1_ref, gate_ref, gf_ref, out_ref, *, final_norm):
    y = x_ref[...] + gate_ref[...] * (y0_ref[...] + y1_ref[...])
    if final_norm:
        y = y * lax.rsqrt(jnp.mean(y * y, axis=-1, keepdims=True) + NORM_EPS) * gf_ref[...]
    out_ref[...] = y


def _combine(x, y, row0, n_all, mod, layer, g_final, final_norm):
    nb, r, d = x.shape
    tm = _largest_tile(r, 128, 8)
    assert row0 % tm == 0 and n_all % tm == 0
    xspec = pl.BlockSpec((None, tm, d), lambda b, i: (b, i, 0))
    yspec = lambda k: pl.BlockSpec((tm, d), lambda b, i: ((k * n_all + row0) // tm + b * (r // tm) + i, 0))
    return pl.pallas_call(
        functools.partial(_combine_kernel, final_norm=final_norm),
        grid=(nb, r // tm),
        in_specs=[xspec, yspec(0), yspec(1), _mod_spec(mod, layer, 5, tm, d),
                  pl.BlockSpec((1, d), lambda b, i: (0, 0))],
        out_specs=xspec,
        out_shape=jax.ShapeDtypeStruct((nb, r, d), F32),
        compiler_params=_cparams("parallel", "parallel"),
        name="combine",
    )(x, y, y, mod, g_final)


def _kv_rows(k, v, heads, hd):
    return jnp.stack([k, v], axis=-2).reshape(k.shape[:-1] + (2, heads, hd))


def kernel(x_prompt, x_sample, cache_a0_kv, cache_a1_kv, cache_a2_kv, cache_b_kv, cache_c_kv, page_table,
           c_prompt, c_sample, rel_table, w_ada, b_ada, g_mix, g_ffn, g_final, w_in_a, w_out_a, w_in_b, sinks_b,
           w_out_b, w_in_c, lambda_c, subln_c, w_out_c, w_grp, b_grp, w_rt, b_rt, w_gate, w_up, w_down):
    depth, d = g_mix.shape
    bp, sp, _ = x_prompt.shape
    bs, ts, _ = x_sample.shape
    assert ts == 1
    a_caches_t = [c.transpose(0, 1, 3, 4, 5, 2) for c in (cache_a0_kv, cache_a1_kv, cache_a2_kv)]
    cache_b_t = cache_b_kv.transpose(0, 1, 3, 4, 5, 2)
    table = rel_table.astype(F32)

    m_all = _ada_all(jnp.concatenate([c_prompt, c_sample], axis=0), w_ada, b_ada)
    mod_p = m_all[:, :bp].reshape(depth, bp, 1, 6 * d)
    mod_s = m_all[:, bp:].reshape(depth, 1, bs, 6 * d)
    g_mix3 = g_mix.reshape(depth, 1, d)
    g_ffn3 = g_ffn.reshape(depth, 1, d)
    g_fin = g_final.reshape(1, d)

    w_in = [w_in_a.astype(BF16), w_in_b.astype(BF16), w_in_c.astype(BF16)]
    w_out = [w_out_a.astype(BF16), w_out_b.astype(BF16), w_out_c.astype(BF16)]
    wr = jnp.concatenate([w_grp, w_rt.transpose(0, 2, 1, 3).reshape(depth, d, N_EXPERTS)], axis=-1)
    wr = jnp.pad(wr, ((0, 0), (0, 0), (0, ROUTER_LANES - wr.shape[-1])))
    br = jnp.concatenate([b_grp, b_rt.reshape(depth, N_EXPERTS)], axis=-1)
    br = jnp.pad(br, ((0, 0), (0, ROUTER_LANES - br.shape[-1]))).reshape(depth, 1, ROUTER_LANES)

    a_bias = [_band_bias(table, r, jnp.arange(A_HEADS)) for r in A_DILATIONS]
    wa = A_HEADS * A_HD

    xp = x_prompt
    xs = x_sample.reshape(1, bs, d)
    a_p = [[] for _ in range(A_GROUPS)]
    a_new = [None] * A_GROUPS
    b_p, b_s, c_p, c_s = [], [], [], []
    for i in range(depth):
        kind, li = i % N_MIXERS, i // N_MIXERS
        qkv_p = _norm_mod_matmul(xp, g_mix3, mod_p, i, w_in[kind][li])
        qkv_s = _norm_mod_matmul(xs, g_mix3, mod_s, i, w_in[kind][li])[0]
        if kind == 0:
            res_p = [_a_prompt_group(qkv_p, g, a_bias[g]) for g in range(A_GROUPS)]
            res_s = [_a_sample_group(qkv_s, a_caches_t[g], a_new[g], li, g, table) for g in range(A_GROUPS)]
            a_new = [r[2] for r in res_s]
            xp = _proj_res([r[0] for r in res_p], [r[1] for r in res_p], w_out[0][li], xp, mod_p, i)
            xs = _proj_res([r[0][None] for r in res_s], [r[1][None] for r in res_s], w_out[0][li], xs, mod_s, i)
            for g, L in enumerate(A_WINDOWS):
                lp = min(L, sp)
                kc, vc = (A_GROUPS + g) * wa, (2 * A_GROUPS + g) * wa
                a_p[g].append(_kv_rows(qkv_p[:, sp - lp:, kc:kc + wa], qkv_p[:, sp - lp:, vc:vc + wa], A_HEADS, A_HD))
        elif kind == 1:
            o_p = _b_prompt(qkv_p, sinks_b[li], table)
            o_s = _b_sample(qkv_s, cache_b_t, li, sinks_b[li], table)
            xp = _proj_res([o_p], [], w_out[1][li], xp, mod_p, i)
            xs = _proj_res([o_s[None]], [], w_out[1][li], xs, mod_s, i)
            nq, nk = B_HEADS * B_HD, B_KV_HEADS * B_HD
            lp = min(B_WINDOW, sp)
            b_p.append(_kv_rows(qkv_p[:, sp - lp:, nq:nq + nk], qkv_p[:, sp - lp:, nq + nk:], B_KV_HEADS, B_HD))
            b_s.append(_kv_rows(qkv_s[:, nq:nq + nk], qkv_s[:, nq + nk:], B_KV_HEADS, B_HD))
        else:
            lam_init = 0.8 - 0.6 * math.exp(-0.3 * i)
            o_p = _c_prompt(qkv_p, lambda_c[li], subln_c[li], table, lam_init)
            o_s = _c_sample(qkv_s, cache_c_kv, li, page_table, lambda_c[li], subln_c[li], table, lam_init)
            xp = _proj_res([o_p], [], w_out[2][li], xp, mod_p, i)
            xs = _proj_res([o_s[None]], [], w_out[2][li], xs, mod_s, i)
            nq, nk = C_HEADS * C_VD, C_KV_HEADS * C_VD
            c_p.append(_kv_rows(qkv_p[:, :, nq:nq + nk], qkv_p[:, :, nq + nk:], C_KV_HEADS, C_VD))
            c_s.append(_kv_rows(qkv_s[:, None, nq:nq + nk], qkv_s[:, None, nq + nk:], C_KV_HEADS, C_VD))
        last = i == depth - 1
        n_p, n_all = bp * sp, bp * sp + bs
        h_all, sel = _router(xp, xs, g_ffn3, mod_p, mod_s, i, wr, br)
        y = _experts(h_all, sel, w_gate, w_up, w_down, i)
        xp = _combine(xp, y, 0, n_all, mod_p, i, g_fin, last)
        xs = _combine(xs, y, n_p, n_all, mod_s, i, g_fin, last)

    def shifted(cache, new_rows):
        return jnp.concatenate([cache[:, :, 1:], jnp.stack(new_rows)[:, :, None]], axis=2)

    a_out = [c.transpose(0, 1, 5, 2, 3, 4) for c in a_new]
    return (xp, xs.reshape(bs, ts, d),
            jnp.stack(a_p[0]), a_out[0],
            jnp.stack(a_p[1]), a_out[1],
            jnp.stack(a_p[2]), a_out[2],
            jnp.stack(b_p), shifted(cache_b_kv, b_s),
            jnp.stack(c_p), jnp.stack(c_s))
```

```python
import functools
import math

import jax
import jax.numpy as jnp
from jax import lax
from jax.experimental import pallas as pl
from jax.experimental.pallas import tpu as pltpu

F32 = jnp.float32
BF16 = jnp.bfloat16

N_MIXERS = 3
A_WINDOWS = (128, 512, 2048)
A_DILATIONS = (1, 4, 16)
A_GROUPS = 3
A_HEADS = 8
A_HD = 64
B_WINDOW = 128
B_HEADS = 16
B_KV_HEADS = 2
B_GROUP = B_HEADS // B_KV_HEADS
B_HD = 64
C_HEADS = 8
C_KV_HEADS = 4
C_GROUP = C_HEADS // C_KV_HEADS
C_HD = 64
C_VD = 2 * C_HD
NUM_BUCKETS = 32
MAX_DISTANCE = 2048
N_GROUPS = 4
EXPERTS_PER_GROUP = 8
N_EXPERTS = N_GROUPS * EXPERTS_PER_GROUP
PAGE_SIZE = 128
QBLOCK = 128
NORM_EPS = 1e-6
NEG = -1e30

LANES = 128
ROUTER_LANES = LANES
VMEM_LIMIT = 56 * 1024 * 1024


def _cparams(*sem):
    return pltpu.CompilerParams(dimension_semantics=sem, vmem_limit_bytes=VMEM_LIMIT)


def _largest_tile(n, cap, mult):
    best = None
    for t in range(mult, min(n, cap) + 1, mult):
        if n % t == 0:
            best = t
    assert best is not None, (n, cap, mult)
    return best


def _t5_bucket(dist):
    max_exact = NUM_BUCKETS // 2
    d = jnp.maximum(dist, 0)
    ratio = jnp.log(jnp.maximum(d, 1).astype(F32) / max_exact) / math.log(MAX_DISTANCE / max_exact)
    large = jnp.minimum(max_exact + (ratio * (NUM_BUCKETS - max_exact)).astype(jnp.int32), NUM_BUCKETS - 1)
    return jnp.where(d < max_exact, d, large)


def _rel_bias(table, dist):
    return table.astype(F32)[_t5_bucket(dist)]


def _toeplitz(w, rows, cols):
    m = w.shape[-1]
    assert m >= rows + cols - 1
    flat = jnp.tile(w, (1,) * (w.ndim - 1) + (rows,))[..., :rows * (m - 1)]
    return flat.reshape(w.shape[:-1] + (rows, m - 1))[..., :cols]


def _band_bias(table, dil, head_cols):
    m = 3 * QBLOCK
    u = jnp.arange(m, dtype=jnp.int32)
    u = jnp.where(u < 2 * QBLOCK, u, u - m)
    dist = QBLOCK - u
    ok = (dist >= 0) & (dist <= QBLOCK)
    w = jnp.where(ok[None], _rel_bias(table, dil * dist)[:, head_cols].T, NEG)
    return _toeplitz(w, QBLOCK, 2 * QBLOCK)


def _ada_kernel(c_ref, w_ref, b_ref, o_ref):
    c = c_ref[...]
    a = c * jax.nn.sigmoid(c)
    o_ref[...] = jnp.dot(a.astype(BF16), w_ref[...].astype(BF16), preferred_element_type=F32) + b_ref[...]


def _ada_all(c_all, w_ada, b_ada):
    depth, d, f = w_ada.shape
    n = c_all.shape[0]
    tn = _largest_tile(f, 1024, LANES)
    return pl.pallas_call(
        _ada_kernel,
        grid=(depth, f // tn),
        in_specs=[pl.BlockSpec((n, d), lambda i, j: (0, 0)),
                  pl.BlockSpec((None, d, tn), lambda i, j: (i, 0, j)),
                  pl.BlockSpec((None, 1, tn), lambda i, j: (i, 0, j))],
        out_specs=pl.BlockSpec((None, n, tn), lambda i, j: (i, 0, j)),
        out_shape=jax.ShapeDtypeStruct((depth, n, f), F32),
        compiler_params=_cparams("parallel", "parallel"),
        name="ada",
    )(c_all, w_ada, b_ada.reshape(depth, 1, f))


def _mod_spec(mod, layer, chunk, tm, width):
    per_row = mod.shape[2] != 1
    rows = tm if per_row else 1
    per = mod.shape[3] // 6 // width
    return pl.BlockSpec((None, None, rows, width),
                        lambda b, i, *_: (layer, b, i if per_row else 0, chunk * per))


def _norm_mod(x, g, shift, scale):
    y = x * lax.rsqrt(jnp.mean(x * x, axis=-1, keepdims=True) + NORM_EPS) * g
    return y * (1.0 + scale) + shift


def _nmm_kernel(x_ref, g_ref, sh_ref, sc_ref, w_ref, o_ref, h_ref):
    @pl.when(pl.program_id(2) == 0)
    def _():
        h_ref[...] = _norm_mod(x_ref[...], g_ref[...], sh_ref[...], sc_ref[...]).astype(h_ref.dtype)

    o_ref[...] = jnp.dot(h_ref[...], w_ref[...], preferred_element_type=F32)


def _norm_mod_matmul(x, g, mod, layer, w):
    nb, r, d = x.shape
    f = w.shape[1]
    tm = _largest_tile(r, 512, 8)
    tn = _largest_tile(f, 1536, LANES)
    return pl.pallas_call(
        _nmm_kernel,
        grid=(nb, r // tm, f // tn),
        in_specs=[pl.BlockSpec((None, tm, d), lambda b, i, j: (b, i, 0)),
                  pl.BlockSpec((None, 1, d), lambda b, i, j: (layer, 0, 0)),
                  _mod_spec(mod, layer, 0, tm, d),
                  _mod_spec(mod, layer, 1, tm, d),
                  pl.BlockSpec((d, tn), lambda b, i, j: (0, j))],
        out_specs=pl.BlockSpec((None, tm, tn), lambda b, i, j: (b, i, j)),
        out_shape=jax.ShapeDtypeStruct((nb, r, f), F32),
        scratch_shapes=[pltpu.VMEM((tm, d), BF16)],
        compiler_params=_cparams("parallel", "parallel", "arbitrary"),
        name="norm_qkv",
    )(x, g, mod, mod, w)


def _proj_res_kernel(o_ref, w_ref, x_ref, gate_ref, out_ref):
    y = jnp.dot(o_ref[...].astype(BF16), w_ref[...], preferred_element_type=F32)
    out_ref[...] = x_ref[...] + gate_ref[...] * y


def _merge_proj_res_kernel(o0, o1, o2, l0, l1, l2, w_ref, x_ref, gate_ref, out_ref):
    a0, a1, a2 = l0[...], l1[...], l2[...]
    m = jnp.maximum(jnp.maximum(a0, a1), a2)
    e0, e1, e2 = jnp.exp(a0 - m), jnp.exp(a1 - m), jnp.exp(a2 - m)
    o = (e0 * o0[...] + e1 * o1[...] + e2 * o2[...]) / (e0 + e1 + e2)
    y = jnp.dot(o.astype(BF16), w_ref[...], preferred_element_type=F32)
    out_ref[...] = x_ref[...] + gate_ref[...] * y


def _proj_res(os_, lses, w, x, mod, layer):
    nb, r, d = x.shape
    k = w.shape[0]
    tm = _largest_tile(r, 512, 8)
    ospec = pl.BlockSpec((None, tm, k), lambda b, i: (b, i, 0))
    xspec = pl.BlockSpec((None, tm, d), lambda b, i: (b, i, 0))
    ins = list(os_) + list(lses)
    kern = _merge_proj_res_kernel if lses else _proj_res_kernel
    return pl.pallas_call(
        kern,
        grid=(nb, r // tm),
        in_specs=[ospec] * len(ins) + [pl.BlockSpec((k, d), lambda b, i: (0, 0)), xspec,
                                       _mod_spec(mod, layer, 2, tm, d)],
        out_specs=xspec,
        out_shape=jax.ShapeDtypeStruct((nb, r, d), F32),
        compiler_params=_cparams("parallel", "parallel"),
        name="proj_res",
    )(*ins, w, x, mod)


def _a_prompt_kernel(q_ref, kp_ref, kc_ref, vp_ref, vc_ref, bias_ref, o_ref, l_ref):
    bi = pl.program_id(2)
    scale = A_HD ** -0.5
    q = q_ref[...].astype(BF16)
    k = jnp.concatenate([kp_ref[...], kc_ref[...]], axis=0).astype(BF16)
    v = jnp.concatenate([vp_ref[...], vc_ref[...]], axis=0).astype(BF16)
    col = lax.broadcasted_iota(jnp.int32, (QBLOCK, 2 * QBLOCK), 1)
    keep = (col >= QBLOCK) | (bi > 0)
    outs, lses = [], []
    for h in range(A_HEADS):
        sl = slice(h * A_HD, (h + 1) * A_HD)
        s = lax.dot_general(q[:, sl], k[:, sl], (((1,), (1,)), ((), ())), preferred_element_type=F32)
        s = jnp.where(keep, s * scale + bias_ref[h], NEG)
        m = jnp.max(s, axis=-1, keepdims=True)
        p = jnp.exp(s - m)
        l = jnp.sum(p, axis=-1, keepdims=True)
        outs.append(jnp.dot(p.astype(BF16), v[:, sl], preferred_element_type=F32) / l)
        lses.append(jnp.broadcast_to(m + jnp.log(l), (QBLOCK, A_HD)))
    o_ref[...] = jnp.concatenate(outs, axis=1)
    l_ref[...] = jnp.concatenate(lses, axis=1)


def _a_dilated_kernel(q_ref, kp_ref, kc_ref, vp_ref, vc_ref, bias_ref, o_ref, l_ref, *, dil):
    bi = pl.program_id(1)
    scale = A_HD ** -0.5
    col = lax.broadcasted_iota(jnp.int32, (QBLOCK, 2 * QBLOCK), 1)
    keep = (col >= QBLOCK) | (bi > 0)

    def one_class(c, carry):
        rows = pl.ds(c, QBLOCK, stride=dil)
        q = (q_ref[rows, :] * scale).astype(BF16)
        k = jnp.concatenate([kp_ref[rows, :], kc_ref[rows, :]], axis=0).astype(BF16)
        v = jnp.concatenate([vp_ref[rows, :], vc_ref[rows, :]], axis=0).astype(BF16)
        outs, lses = [], []
        for h in range(LANES // A_HD):
            sl = slice(h * A_HD, (h + 1) * A_HD)
            s = lax.dot_general(q[:, sl], k[:, sl], (((1,), (1,)), ((), ())), preferred_element_type=F32)
            s = jnp.where(keep, s + bias_ref[h], NEG)
            m = jnp.max(s, axis=-1, keepdims=True)
            p = jnp.exp(s - m)
            l = jnp.sum(p, axis=-1, keepdims=True)
            outs.append(jnp.dot(p.astype(BF16), v[:, sl], preferred_element_type=F32) / l)
            lses.append(jnp.broadcast_to(m + jnp.log(l), (QBLOCK, A_HD)))
        o_ref[rows, :] = jnp.concatenate(outs, axis=1)
        l_ref[rows, :] = jnp.concatenate(lses, axis=1)
        return carry

    lax.fori_loop(0, dil, one_class, 0, unroll=4)


def _a_dilated_group(qkv, g, bias):
    b, t, f = qkv.shape
    dil = A_DILATIONS[g]
    w = A_HEADS * A_HD
    chunk = QBLOCK * dil
    hpb = w // LANES
    hpl = LANES // A_HD

    def spec(sec, prev):
        return pl.BlockSpec((None, chunk, LANES),
                            lambda bb, i, hp: (bb, jnp.maximum(i - 1, 0) if prev else i,
                                               (sec * A_GROUPS + g) * hpb + hp))

    ospec = pl.BlockSpec((None, chunk, LANES), lambda bb, i, hp: (bb, i, hp))
    oshape = jax.ShapeDtypeStruct((b, t, w), F32)
    return pl.pallas_call(
        functools.partial(_a_dilated_kernel, dil=dil),
        grid=(b, t // chunk, hpb),
        in_specs=[spec(0, False), spec(1, True), spec(1, False), spec(2, True), spec(2, False),
                  pl.BlockSpec((hpl, QBLOCK, 2 * QBLOCK), lambda bb, i, hp: (hp, 0, 0))],
        out_specs=[ospec, ospec],
        out_shape=[oshape, oshape],
        compiler_params=_cparams("parallel", "parallel", "parallel"),
        name=f"a_prompt_g{g}",
    )(qkv, qkv, qkv, qkv, qkv, bias)


def _a_prompt_group(qkv, g, bias):
    if A_DILATIONS[g] > 1:
        return _a_dilated_group(qkv, g, bias)
    b, t, f = qkv.shape
    dil = A_DILATIONS[g]
    lr = t // dil
    nq = lr // QBLOCK
    w = A_HEADS * A_HD
    per = f // w
    view = qkv.reshape(b, lr, dil * f)

    def spec(sec, prev):
        return pl.BlockSpec((None, QBLOCK, w),
                            lambda bb, c, i: (bb, jnp.maximum(i - 1, 0) if prev else i, c * per + sec * A_GROUPS + g))

    ospec = pl.BlockSpec((None, QBLOCK, w), lambda bb, c, i: (bb, i, c))
    oshape = jax.ShapeDtypeStruct((b, lr, dil * w), F32)
    o, l = pl.pallas_call(
        _a_prompt_kernel,
        grid=(b, dil, nq),
        in_specs=[spec(0, False), spec(1, True), spec(1, False), spec(2, True), spec(2, False),
                  pl.BlockSpec((A_HEADS, QBLOCK, 2 * QBLOCK), lambda bb, c, i: (0, 0, 0))],
        out_specs=[ospec, ospec],
        out_shape=[oshape, oshape],
        compiler_params=_cparams("parallel", "parallel", "parallel"),
        name=f"a_prompt_g{g}",
    )(view, view, view, view, view, bias)
    return o.reshape(b, t, w), l.reshape(b, t, w)


def _decode_head(kt, vt, q, kn, vn, bias, bias_new, scale, sink=None):
    s = jnp.sum(kt * q, axis=1, keepdims=True) * scale + bias
    s_new = jnp.sum(kn * q, axis=1, keepdims=True) * scale + bias_new
    m = jnp.maximum(jnp.max(s, axis=-1, keepdims=True), s_new)
    if sink is not None:
        m = jnp.maximum(m, sink)
    p = jnp.exp(s - m)
    pn = jnp.exp(s_new - m)
    l = jnp.sum(p, axis=-1, keepdims=True) + pn
    if sink is not None:
        l = l + jnp.exp(sink - m)
    o = (jnp.sum(vt * p, axis=-1, keepdims=True) + pn * vn) / l
    return o, m + jnp.log(l)


def _a_sample_kernel(q_ref, kvn_ref, c_ref, bias_ref, bn_ref, *rest):
    o_ref, l_ref, out_ref = rest[-3:]
    L = c_ref.shape[-1]
    scale = A_HD ** -0.5
    last = lax.broadcasted_iota(jnp.int32, (1, 1, L), 2) == L - 1
    for h in range(A_HEADS):
        kt, vt = c_ref[:, 0, h], c_ref[:, 1, h]
        kn, vn = kvn_ref[:, 0, h], kvn_ref[:, 1, h]
        o, lse = _decode_head(kt, vt, q_ref[:, h], kn, vn, bias_ref[h], bn_ref[h], scale)
        o_ref[:, h] = o
        l_ref[:, h] = lse
        out_ref[:, 0, h] = jnp.where(last, kn, pltpu.roll(kt, L - 1, 2))
        out_ref[:, 1, h] = jnp.where(last, vn, pltpu.roll(vt, L - 1, 2))


def _a_sample_group(qkv, cache_t, prev_out, li, g, table):
    n, f = qkv.shape
    dil = A_DILATIONS[g]
    n_lay, nb, two, hh, hd, L = cache_t.shape
    assert (nb, two, hh, hd) == (n, 2, A_HEADS, A_HD) and L == A_WINDOWS[g]
    w = A_HEADS * A_HD
    col = lambda sec: qkv[:, (sec * A_GROUPS + g) * w:(sec * A_GROUPS + g + 1) * w]
    q = col(0).reshape(n, A_HEADS, A_HD, 1)
    kvn = jnp.stack([col(1), col(2)], axis=1).reshape(n, 2, A_HEADS, A_HD, 1)
    dist = L - jnp.arange(L, dtype=jnp.int32)
    bias = jnp.where((dist % dil == 0)[None], _rel_bias(table, dist).T, NEG).reshape(A_HEADS, 1, L)
    bn = _rel_bias(table, jnp.zeros((1,), jnp.int32)).T.reshape(A_HEADS, 1, 1)
    bt = max(1, (4 * 1024 * 1024) // (2 * w * L * 4))
    cspec = pl.BlockSpec((None, bt, 2, A_HEADS, A_HD, L), lambda b: (li, b, 0, 0, 0, 0))
    in_specs = [pl.BlockSpec((bt, A_HEADS, A_HD, 1), lambda b: (b, 0, 0, 0)),
                pl.BlockSpec((bt, 2, A_HEADS, A_HD, 1), lambda b: (b, 0, 0, 0, 0)),
                cspec,
                pl.BlockSpec((A_HEADS, 1, L), lambda b: (0, 0, 0)),
                pl.BlockSpec((A_HEADS, 1, 1), lambda b: (0, 0, 0))]
    args = [q, kvn, cache_t, bias, bn]
    aliases = {}
    if prev_out is not None:
        in_specs.append(pl.BlockSpec(memory_space=pl.ANY))
        args.append(prev_out)
        aliases = {len(args) - 1: 2}
    o, lse, out = pl.pallas_call(
        _a_sample_kernel,
        grid=(n // bt,),
        in_specs=in_specs,
        out_specs=[pl.BlockSpec((bt, A_HEADS, A_HD, 1), lambda b: (b, 0, 0, 0)),
                   pl.BlockSpec((bt, A_HEADS, 1, 1), lambda b: (b, 0, 0, 0)),
                   cspec],
        out_shape=[jax.ShapeDtypeStruct((n, A_HEADS, A_HD, 1), F32),
                   jax.ShapeDtypeStruct((n, A_HEADS, 1, 1), F32),
                   jax.ShapeDtypeStruct(cache_t.shape, F32)],
        input_output_aliases=aliases,
        compiler_params=_cparams("parallel"),
        name=f"a_sample_g{g}",
    )(*args)
    lse = jnp.broadcast_to(lse.reshape(n, A_HEADS, 1), (n, A_HEADS, A_HD)).reshape(n, w)
    return o.reshape(n, w), lse, out


def _head_mask(rows, width, hd, lane_head_of_row):
    lane = lax.broadcasted_iota(jnp.int32, (rows, width), 1) // hd
    row = lax.broadcasted_iota(jnp.int32, (rows, width), 0)
    return (lane == lane_head_of_row(row)).astype(F32)


def _b_prompt_kernel(q_ref, kp_ref, kc_ref, vp_ref, vc_ref, bias_ref, sink_ref, o_ref):
    bi = pl.program_id(1)
    scale = B_HD ** -0.5
    q = q_ref[...].astype(BF16)
    k = jnp.concatenate([kp_ref[...], kc_ref[...]], axis=0).astype(BF16)
    v = jnp.concatenate([vp_ref[...], vc_ref[...]], axis=0).astype(BF16)
    col = lax.broadcasted_iota(jnp.int32, (B_GROUP * QBLOCK, 2 * QBLOCK), 1)
    keep = (col >= QBLOCK) | (bi > 0)
    outs = []
    for kh in range(B_KV_HEADS):
        qs = jnp.concatenate([q[:, (kh * B_GROUP + g) * B_HD:(kh * B_GROUP + g + 1) * B_HD]
                              for g in range(B_GROUP)], axis=0)
        sl = slice(kh * B_HD, (kh + 1) * B_HD)
        s = lax.dot_general(qs, k[:, sl], (((1,), (1,)), ((), ())), preferred_element_type=F32)
        s = jnp.where(keep, s * scale + bias_ref[kh], NEG)
        sink = sink_ref[kh]
        m = jnp.maximum(jnp.max(s, axis=-1, keepdims=True), sink)
        p = jnp.exp(s - m)
        den = jnp.sum(p, axis=-1, keepdims=True) + jnp.exp(sink - m)
        o = jnp.dot(p.astype(BF16), v[:, sl], preferred_element_type=F32) / den
        outs += [o[g * QBLOCK:(g + 1) * QBLOCK] for g in range(B_GROUP)]
    o_ref[...] = jnp.concatenate(outs, axis=1)


def _b_prompt(qkv, sinks, table):
    b, t, f = qkv.shape
    nq = t // QBLOCK
    wq = B_HEADS * B_HD
    wk = B_KV_HEADS * B_HD
    head_cols = jnp.arange(B_HEADS) // (B_HEADS // table.shape[1])
    bias = _band_bias(table, 1, head_cols).reshape(B_KV_HEADS, B_GROUP * QBLOCK, 2 * QBLOCK)
    sink = jnp.broadcast_to(sinks.astype(F32).reshape(B_KV_HEADS, B_GROUP, 1, 1),
                            (B_KV_HEADS, B_GROUP, QBLOCK, 1)).reshape(B_KV_HEADS, B_GROUP * QBLOCK, 1)

    def kvspec(colblock, prev):
        return pl.BlockSpec((None, QBLOCK, wk), lambda bb, i: (bb, jnp.maximum(i - 1, 0) if prev else i, colblock))

    kcol = wq // wk
    return pl.pallas_call(
        _b_prompt_kernel,
        grid=(b, nq),
        in_specs=[pl.BlockSpec((None, QBLOCK, wq), lambda bb, i: (bb, i, 0)),
                  kvspec(kcol, True), kvspec(kcol, False), kvspec(kcol + 1, True), kvspec(kcol + 1, False),
                  pl.BlockSpec(bias.shape, lambda bb, i: (0, 0, 0)),
                  pl.BlockSpec(sink.shape, lambda bb, i: (0, 0, 0))],
        out_specs=pl.BlockSpec((None, QBLOCK, wq), lambda bb, i: (bb, i, 0)),
        out_shape=jax.ShapeDtypeStruct((b, t, wq), F32),
        compiler_params=_cparams("parallel", "parallel"),
        name="b_prompt",
    )(qkv, qkv, qkv, qkv, qkv, bias, sink)


def _b_sample_kernel(q_ref, kvn_ref, c_ref, bias_ref, bn_ref, sink_ref, o_ref):
    scale = B_HD ** -0.5
    for hq in range(B_HEADS):
        kh = hq // B_GROUP
        o, _ = _decode_head(c_ref[:, 0, kh], c_ref[:, 1, kh], q_ref[:, hq], kvn_ref[:, 0, kh], kvn_ref[:, 1, kh],
                            bias_ref[hq], bn_ref[hq], scale, sink_ref[hq])
        o_ref[:, hq] = o


def _b_sample(qkv, cache_t, li, sinks, table):
    n, f = qkv.shape
    n_lay, nb, two, kvh, hd, L = cache_t.shape
    assert (nb, two, kvh, hd) == (n, 2, B_KV_HEADS, B_HD) and L == B_WINDOW
    wq = B_HEADS * B_HD
    wk = B_KV_HEADS * B_HD
    head_cols = jnp.arange(B_HEADS) // (B_HEADS // table.shape[1])
    dist = L - jnp.arange(L, dtype=jnp.int32)
    bias = _rel_bias(table, dist)[:, head_cols].T.reshape(B_HEADS, 1, L)
    bn = _rel_bias(table, jnp.zeros((1,), jnp.int32))[:, head_cols].T.reshape(B_HEADS, 1, 1)
    q = qkv[:, :wq].reshape(n, B_HEADS, B_HD, 1)
    kvn = jnp.stack([qkv[:, wq:wq + wk], qkv[:, wq + wk:]], axis=1).reshape(n, 2, B_KV_HEADS, B_HD, 1)
    bt = 8
    o = pl.pallas_call(
        _b_sample_kernel,
        grid=(n // bt,),
        in_specs=[pl.BlockSpec((bt, B_HEADS, B_HD, 1), lambda b: (b, 0, 0, 0)),
                  pl.BlockSpec((bt, 2, B_KV_HEADS, B_HD, 1), lambda b: (b, 0, 0, 0, 0)),
                  pl.BlockSpec((None, bt, 2, B_KV_HEADS, B_HD, L), lambda b: (li, b, 0, 0, 0, 0)),
                  pl.BlockSpec((B_HEADS, 1, L), lambda b: (0, 0, 0)),
                  pl.BlockSpec((B_HEADS, 1, 1), lambda b: (0, 0, 0)),
                  pl.BlockSpec((B_HEADS, 1, 1), lambda b: (0, 0, 0))],
        out_specs=pl.BlockSpec((bt, B_HEADS, B_HD, 1), lambda b: (b, 0, 0, 0)),
        out_shape=jax.ShapeDtypeStruct((n, B_HEADS, B_HD, 1), F32),
        compiler_params=_cparams("parallel"),
        name="b_sample",
    )(q, kvn, cache_t, bias, bn, sinks.astype(F32).reshape(B_HEADS, 1, 1))
    return o.reshape(n, wq)


def _diff_lambda(lp, lam_init):
    return (jnp.exp(jnp.sum(lp[0:1] * lp[1:2], axis=-1, keepdims=True))
            - jnp.exp(jnp.sum(lp[2:3] * lp[3:4], axis=-1, keepdims=True)) + lam_init)


def _diff_out(o1, o2, lam, subln, lam_init):
    d = o1 - lam * o2
    y = d * lax.rsqrt(jnp.mean(d * d, axis=-1, keepdims=True) + NORM_EPS) * subln
    return y * (1.0 - lam_init)


C_BLOCK = 512


def _c_prompt_kernel(qi_tab, ki_tab, q_ref, k_ref, v_ref, bias_ref, lam_ref, subln_ref, o_ref,
                     m_sc, l_sc, acc_sc, *, lam_init):
    t = pl.program_id(2)
    qi = qi_tab[t]
    ki = ki_tab[t]
    scale = C_HD ** -0.5

    @pl.when(ki == 0)
    def _():
        m_sc[...] = jnp.full(m_sc.shape, NEG, F32)
        l_sc[...] = jnp.zeros(l_sc.shape, F32)
        acc_sc[...] = jnp.zeros(acc_sc.shape, F32)

    q = (q_ref[...] * scale).astype(BF16)
    k = k_ref[...].astype(BF16)
    v = v_ref[...].astype(BF16)
    for g in range(C_GROUP):
        bias = bias_ref[g]
        for mp in range(2):
            idx = g * 2 + mp
            qs = q[:, g * C_VD + mp * C_HD:g * C_VD + (mp + 1) * C_HD]
            ks = k[:, mp * C_HD:(mp + 1) * C_HD]
            s = lax.dot_general(qs, ks, (((1,), (1,)), ((), ())), preferred_element_type=F32) + bias
            m_old = m_sc[idx]
            m_new = jnp.maximum(m_old, jnp.max(s, axis=-1, keepdims=True))
            alpha = jnp.exp(m_old - m_new)
            p = jnp.exp(s - m_new)
            l_sc[idx] = alpha * l_sc[idx] + jnp.sum(p, axis=-1, keepdims=True)
            acc_sc[idx] = alpha * acc_sc[idx] + jnp.dot(p.astype(BF16), v, preferred_element_type=F32)
            m_sc[idx] = m_new

    @pl.when(ki == qi)
    def _():
        lam = _diff_lambda(lam_ref[...], lam_init)
        outs = []
        for g in range(C_GROUP):
            o1 = acc_sc[g * 2] / l_sc[g * 2]
            o2 = acc_sc[g * 2 + 1] / l_sc[g * 2 + 1]
            outs.append(_diff_out(o1, o2, lam, subln_ref[...], lam_init))
        o_ref[...] = jnp.concatenate(outs, axis=1)


def _c_prompt(qkv, lam_p, subln, table, lam_init):
    b, t, f = qkv.shape
    blk = C_BLOCK
    nblk = t // blk
    pairs = [(qi, ki) for qi in range(nblk) for ki in range(qi + 1)]
    qi_tab = jnp.asarray([p[0] for p in pairs], jnp.int32)
    ki_tab = jnp.asarray([p[1] for p in pairs], jnp.int32)
    u = jnp.arange(2 * blk, dtype=jnp.int32)
    u = jnp.where(u < blk, u, u - 2 * blk)
    dist = jnp.arange(nblk, dtype=jnp.int32)[:, None] * blk - u[None, :]
    w = jnp.where((dist >= 0)[:, None], _rel_bias(table, dist).transpose(0, 2, 1), NEG)
    bias = _toeplitz(w, blk, blk)
    wq = C_GROUP * C_VD
    qcols = C_HEADS * C_VD
    grid_spec = pltpu.PrefetchScalarGridSpec(
        num_scalar_prefetch=2,
        grid=(b, C_KV_HEADS, len(pairs)),
        in_specs=[pl.BlockSpec((None, blk, wq), lambda bb, kh, tt, qt, kt: (bb, qt[tt], kh)),
                  pl.BlockSpec((None, blk, C_VD), lambda bb, kh, tt, qt, kt: (bb, kt[tt], qcols // C_VD + kh)),
                  pl.BlockSpec((None, blk, C_VD),
                               lambda bb, kh, tt, qt, kt: (bb, kt[tt], qcols // C_VD + C_KV_HEADS + kh)),
                  pl.BlockSpec((None, C_GROUP, blk, blk), lambda bb, kh, tt, qt, kt: (qt[tt] - kt[tt], kh, 0, 0)),
                  pl.BlockSpec((4, C_HD), lambda bb, kh, tt, qt, kt: (0, 0)),
                  pl.BlockSpec((1, C_VD), lambda bb, kh, tt, qt, kt: (0, 0))],
        out_specs=pl.BlockSpec((None, blk, wq), lambda bb, kh, tt, qt, kt: (bb, qt[tt], kh)),
        scratch_shapes=[pltpu.VMEM((2 * C_GROUP, blk, 1), F32), pltpu.VMEM((2 * C_GROUP, blk, 1), F32),
                        pltpu.VMEM((2 * C_GROUP, blk, C_VD), F32)],
    )
    return pl.pallas_call(
        functools.partial(_c_prompt_kernel, lam_init=lam_init),
        grid_spec=grid_spec,
        out_shape=jax.ShapeDtypeStruct((b, t, qcols), F32),
        compiler_params=_cparams("parallel", "parallel", "arbitrary"),
        name="c_prompt",
    )(qi_tab, ki_tab, qkv, qkv, qkv, bias, lam_p.astype(F32), subln.astype(F32).reshape(1, C_VD))


C_PAGES_PER_STEP = 8
C_MAPS = 2 * C_HEADS
C_ROWS = 2 * C_KV_HEADS


def _c_sample_kernel(pt_ref, q_ref, kvn_ref, lam_ref, subln_ref, bias_ref, bn_ref, *rest, lam_init):
    pages = rest[:C_PAGES_PER_STEP]
    o_ref, m_sc, l_sc, acc_sc = rest[C_PAGES_PER_STEP:]
    j = pl.program_id(1)
    scale = C_HD ** -0.5
    hm = _head_mask(C_MAPS, C_VD, C_HD, lambda r: r // C_HEADS)
    qe = jnp.concatenate([q_ref[...]] * 2, axis=-1) * hm
    qb = (qe * scale).astype(BF16)
    row_kh = (lax.broadcasted_iota(jnp.int32, (C_MAPS, 1), 0) % C_HEADS) // C_GROUP

    @pl.when(j == 0)
    def _():
        kvn = kvn_ref[...]
        s_new = jnp.zeros((C_MAPS, 1), F32)
        v_new = jnp.zeros((C_MAPS, C_VD), F32)
        for kh in range(C_KV_HEADS):
            s_kh = jnp.sum(qe * kvn[kh:kh + 1], axis=-1, keepdims=True)
            s_new = jnp.where(row_kh == kh, s_kh, s_new)
            v_new = jnp.where(row_kh == kh, kvn[C_KV_HEADS + kh:C_KV_HEADS + kh + 1], v_new)
        m_sc[...] = s_new * scale + bn_ref[...]
        l_sc[...] = jnp.ones(l_sc.shape, F32)
        acc_sc[...] = v_new

    rows = [pages[pi][...].astype(BF16) for pi in range(C_PAGES_PER_STEP)]
    ss = [lax.dot_general(qb, rows[pi], (((1,), (1,)), ((), ())), preferred_element_type=F32) + bias_ref[pi]
          for pi in range(C_PAGES_PER_STEP)]
    m = m_sc[...]
    m_new = m
    for s in ss:
        m_new = jnp.maximum(m_new, jnp.max(s, axis=-1, keepdims=True))
    alpha = jnp.exp(m - m_new)
    l = alpha * l_sc[...]
    acc = alpha * acc_sc[...]
    for pi in range(C_PAGES_PER_STEP):
        p = jnp.exp(ss[pi] - m_new)
        l = l + jnp.sum(p, axis=-1, keepdims=True)
        pv = pltpu.roll(p, C_KV_HEADS, 1)
        acc = acc + jnp.dot(pv.astype(BF16), rows[pi], preferred_element_type=F32)
    m_sc[...] = m_new
    l_sc[...] = l
    acc_sc[...] = acc

    @pl.when(j == pl.num_programs(1) - 1)
    def _():
        o = acc / l
        lam = _diff_lambda(lam_ref[...], lam_init)
        o_ref[...] = _diff_out(o[:C_HEADS], o[C_HEADS:], lam, subln_ref[...], lam_init)


def _c_sample(qkv, cache, ci, page_table, lam_p, subln, table, lam_init):
    n, f = qkv.shape
    n_lay, n_phys, page, two, kvh, vd = cache.shape
    assert (page, two, kvh, vd) == (PAGE_SIZE, 2, C_KV_HEADS, C_VD)
    n_pages = page_table.shape[1]
    pps = C_PAGES_PER_STEP
    assert n_pages % pps == 0
    qcols = C_HEADS * C_VD
    prow = page * C_ROWS
    view = cache.reshape(n_lay, n_phys, prow, C_VD)
    q16 = qkv[:, :qcols].reshape(n, C_HEADS, 2, C_HD).transpose(0, 2, 1, 3).reshape(n, C_MAPS, C_HD)
    kvn = qkv[:, qcols:].reshape(n, C_ROWS, C_VD)
    n_keys = n_pages * page
    b8 = _rel_bias(table, n_keys - jnp.arange(n_keys, dtype=jnp.int32)).T
    b16 = jnp.concatenate([b8, b8], axis=0)
    row_kh = (jnp.arange(C_MAPS) % C_HEADS) // C_GROUP
    own = row_kh[:, None, None] == jnp.arange(C_ROWS)[None, None, :]
    bias = jnp.where(own, b16[:, :, None], NEG).reshape(C_MAPS, n_pages, prow).transpose(1, 0, 2)
    bn = _rel_bias(table, jnp.zeros((1,), jnp.int32)).T
    bn = jnp.concatenate([bn, bn], axis=0)
    pt_flat = page_table.reshape(-1).astype(jnp.int32)

    def page_spec(pi):
        return pl.BlockSpec((None, None, prow, C_VD),
                            lambda b, j, pt: (ci, pt[b * n_pages + j * pps + pi], 0, 0))

    grid_spec = pltpu.PrefetchScalarGridSpec(
        num_scalar_prefetch=1,
        grid=(n, n_pages // pps),
        in_specs=[pl.BlockSpec((None, C_MAPS, C_HD), lambda b, j, pt: (b, 0, 0)),
                  pl.BlockSpec((None, C_ROWS, C_VD), lambda b, j, pt: (b, 0, 0)),
                  pl.BlockSpec((4, C_HD), lambda b, j, pt: (0, 0)),
                  pl.BlockSpec((1, C_VD), lambda b, j, pt: (0, 0)),
                  pl.BlockSpec((pps, C_MAPS, prow), lambda b, j, pt: (j, 0, 0)),
                  pl.BlockSpec((C_MAPS, 1), lambda b, j, pt: (0, 0))] + [page_spec(pi) for pi in range(pps)],
        out_specs=pl.BlockSpec((None, C_HEADS, C_VD), lambda b, j, pt: (b, 0, 0)),
        scratch_shapes=[pltpu.VMEM((C_MAPS, 1), F32), pltpu.VMEM((C_MAPS, 1), F32), pltpu.VMEM((C_MAPS, C_VD), F32)],
    )
    o = pl.pallas_call(
        functools.partial(_c_sample_kernel, lam_init=lam_init),
        grid_spec=grid_spec,
        out_shape=jax.ShapeDtypeStruct((n, C_HEADS, C_VD), F32),
        compiler_params=_cparams("parallel", "arbitrary"),
        name="c_sample",
    )(pt_flat, q16, kvn, lam_p.astype(F32), subln.astype(F32).reshape(1, C_VD), bias, bn, *([view] * pps))
    return o.reshape(n, qcols)


def _to_tile_rows(ref, x):
    for c in range(ref.shape[1]):
        ref[:, c, :] = x[:, c * LANES:(c + 1) * LANES]


def _from_tile_rows(ref):
    return jnp.concatenate([ref[:, c, :] for c in range(ref.shape[1])], axis=1)


def _route(x, g, shift, scale, wr, br, h_ref, sel_ref):
    h = _norm_mod(x, g, shift, scale)
    _to_tile_rows(h_ref, h)
    logits = jnp.dot(h, wr, preferred_element_type=F32, precision=lax.Precision.HIGHEST) + br
    lane = lax.broadcasted_iota(jnp.int32, logits.shape, 1).astype(F32)
    first = lambda hit: jnp.min(jnp.where(hit, lane, float(ROUTER_LANES)), axis=-1, keepdims=True)
    is_g = lane < N_GROUPS
    gl = jnp.where(is_g, logits, NEG)
    gmax = jnp.max(gl, axis=-1, keepdims=True)
    g_sel = first(is_g & (gl == gmax))
    g_prob = 1.0 / jnp.sum(jnp.where(is_g, jnp.exp(gl - gmax), 0.0), axis=-1, keepdims=True)
    lo = N_GROUPS + EXPERTS_PER_GROUP * g_sel
    in_grp = (lane >= lo) & (lane < lo + EXPERTS_PER_GROUP)
    el = jnp.where(in_grp, logits, NEG)
    v1 = jnp.max(el, axis=-1, keepdims=True)
    i1 = first(in_grp & (el == v1))
    others = in_grp & (lane != i1)
    el2 = jnp.where(others, el, NEG)
    v2 = jnp.max(el2, axis=-1, keepdims=True)
    i2 = first(others & (el2 == v2))
    e2 = jnp.exp(v2 - v1)
    w1 = g_prob / (1.0 + e2)
    w2 = g_prob * e2 / (1.0 + e2)
    sel = jnp.where(lane == 0.0, i1 - N_GROUPS, jnp.where(lane == 1.0, i2 - N_GROUPS, 0.0))
    sel_ref[...] = jnp.where(lane == 2.0, w1, jnp.where(lane == 3.0, w2, sel))


def _router_kernel(xp_ref, xs_ref, g_ref, shp_ref, scp_ref, shs_ref, scs_ref, wr_ref, br_ref, h_ref, sel_ref,
                   *, prompt_tiles):
    i = pl.program_id(0)

    @pl.when(i < prompt_tiles)
    def _():
        _route(xp_ref[...], g_ref[...], shp_ref[...], scp_ref[...], wr_ref[...], br_ref[...], h_ref, sel_ref)

    @pl.when(i >= prompt_tiles)
    def _():
        _route(xs_ref[...], g_ref[...], shs_ref[...], scs_ref[...], wr_ref[...], br_ref[...], h_ref, sel_ref)


def _router(xp, xs, g, mod_p, mod_s, layer, wr, br):
    bp, sp, d = xp.shape
    _, bs, _ = xs.shape
    tm = math.gcd(math.gcd(sp, bs), LANES)
    tpb = sp // tm
    pt = bp * tpb
    st = bs // tm
    n_all = bp * sp + bs
    pidx = lambda i: jnp.minimum(i, pt - 1)
    sidx = lambda i: jnp.maximum(i - pt, 0)
    mp = lambda chunk: pl.BlockSpec((None, None, 1, d), lambda i: (layer, pidx(i) // tpb, 0, chunk))
    ms = lambda chunk: pl.BlockSpec((None, None, tm, d), lambda i: (layer, 0, sidx(i), chunk))
    return pl.pallas_call(
        functools.partial(_router_kernel, prompt_tiles=pt),
        grid=(pt + st,),
        in_specs=[pl.BlockSpec((None, tm, d), lambda i: (pidx(i) // tpb, pidx(i) % tpb, 0)),
                  pl.BlockSpec((None, tm, d), lambda i: (0, sidx(i), 0)),
                  pl.BlockSpec((None, 1, d), lambda i: (layer, 0, 0)),
                  mp(3), mp(4), ms(3), ms(4),
                  pl.BlockSpec((None, d, ROUTER_LANES), lambda i: (layer, 0, 0)),
                  pl.BlockSpec((None, 1, ROUTER_LANES), lambda i: (layer, 0, 0))],
        out_specs=[pl.BlockSpec((tm, d // LANES, LANES), lambda i: (i, 0, 0)),
                   pl.BlockSpec((tm, ROUTER_LANES), lambda i: (i, 0))],
        out_shape=[jax.ShapeDtypeStruct((n_all, d // LANES, LANES), F32),
                   jax.ShapeDtypeStruct((n_all, ROUTER_LANES), F32)],
        compiler_params=_cparams("parallel"),
        name="router",
    )(xp, xs, g, mod_p, mod_p, mod_s, mod_s, wr, br)


MOE_TM = 256
TOP_K = 2


def _route_slots(sel, tm):
    n = sel.shape[0]
    n_pairs = TOP_K * n
    n_tiles = (n_pairs + N_EXPERTS * (tm - 1)) // tm + 1
    p_max = n_tiles * tm
    eid = sel[:, :TOP_K].astype(jnp.int32).T.reshape(-1)
    onehot = (eid[:, None] == jnp.arange(N_EXPERTS, dtype=jnp.int32)[None, :]).astype(jnp.int32)
    cums = jnp.cumsum(onehot, axis=0)
    rank = jnp.sum(cums * onehot, axis=1) - 1
    counts = cums[-1]
    padded = ((counts + tm - 1) // tm) * tm
    ends = jnp.cumsum(padded)
    pos = (ends - padded)[eid] + rank
    pair = jnp.arange(n_pairs, dtype=jnp.int32)
    slot = jnp.arange(p_max, dtype=jnp.int32)
    dst = (n_pairs + slot % (2 * tm)).at[pos].set(pair)
    tile_start = jnp.arange(n_tiles, dtype=jnp.int32) * tm
    tile_expert = jnp.minimum(jnp.sum((ends[None, :] <= tile_start[:, None]).astype(jnp.int32), axis=1),
                              N_EXPERTS - 1)
    n_used = (ends[-1] // tm).reshape(1)
    return tile_expert, dst, n_used


DMA_ISSUE_UNROLL = 8


def _experts_kernel(te_ref, dst_ref, nu_ref, h_hbm, wg_ref, wu_ref, wd_ref, y_hbm,
                    xbuf, obuf, wgb, wub, wdb, gsem, ssem):
    t = pl.program_id(0)
    n_used = nu_ref[0]
    slot = t % 2
    tm = xbuf.shape[1]
    n = h_hbm.shape[0]
    n_pairs = y_hbm.shape[0] - 2 * tm

    def gather(tile, s):
        def body(r, c):
            pair = dst_ref[tile * tm + r]
            token = jnp.where(pair >= n, pair - n, pair)
            token = jnp.where(pair >= n_pairs, 0, token)
            pltpu.make_async_copy(h_hbm.at[pl.ds(token, 1)], xbuf.at[s, pl.ds(r, 1)], gsem.at[s]).start()
            return c
        lax.fori_loop(0, tm, body, 0, unroll=DMA_ISSUE_UNROLL)

    def scatter(tile, s):
        def body(r, c):
            pltpu.make_async_copy(obuf.at[s, pl.ds(r, 1)], y_hbm.at[pl.ds(dst_ref[tile * tm + r], 1)],
                                  ssem.at[s]).start()
            return c
        lax.fori_loop(0, tm, body, 0, unroll=DMA_ISSUE_UNROLL)

    def wait_gather(s):
        pltpu.make_async_copy(h_hbm.at[pl.ds(0, tm)], xbuf.at[s], gsem.at[s]).wait()

    def wait_scatter(s):
        pltpu.make_async_copy(obuf.at[s], y_hbm.at[pl.ds(0, tm)], ssem.at[s]).wait()

    @pl.when(t == 0)
    def _():
        obuf[0] = jnp.zeros(obuf.shape[1:], F32)
        for half in range(2):
            fill = pltpu.make_async_copy(obuf.at[0], y_hbm.at[pl.ds(n_pairs + half * tm, tm)], ssem.at[0])
            fill.start()
            fill.wait()
        gather(0, 0)

    @pl.when(t < n_used)
    def _():
        wait_gather(slot)

        @pl.when(t + 1 < n_used)
        def _():
            gather(t + 1, 1 - slot)

        @pl.when((t == 0) | (te_ref[t] != te_ref[jnp.maximum(t - 1, 0)]))
        def _():
            wgb[...] = wg_ref[...].astype(BF16)
            wub[...] = wu_ref[...].astype(BF16)
            wdb[...] = wd_ref[...].astype(BF16)

        @pl.when(t >= 2)
        def _():
            wait_scatter(slot)

        x = _from_tile_rows(xbuf.at[slot]).astype(BF16)
        a = jnp.dot(x, wgb[...], preferred_element_type=F32)
        u = jnp.dot(x, wub[...], preferred_element_type=F32)
        hid = (a * jax.nn.sigmoid(a)) * u
        _to_tile_rows(obuf.at[slot], jnp.dot(hid.astype(BF16), wdb[...], preferred_element_type=F32))
        scatter(t, slot)

        @pl.when(t == n_used - 1)
        def _():
            @pl.when(t >= 1)
            def _():
                wait_scatter(1 - slot)
            wait_scatter(slot)


def _experts(h, sel, wg, wu, wd, layer):
    n, s8, _ = h.shape
    d = s8 * LANES
    de = wg.shape[3]
    tm = MOE_TM
    tile_expert, dst, n_used = _route_slots(sel, tm)
    n_tiles = tile_expert.shape[0]
    wspec = lambda a, b: pl.BlockSpec((None, None, a, b), lambda t, te, d_, nu: (layer, te[t], 0, 0))
    grid_spec = pltpu.PrefetchScalarGridSpec(
        num_scalar_prefetch=3,
        grid=(n_tiles,),
        in_specs=[pl.BlockSpec(memory_space=pl.ANY),
                  wspec(d, de), wspec(d, de), wspec(de, d)],
        out_specs=pl.BlockSpec(memory_space=pl.ANY),
        scratch_shapes=[pltpu.VMEM((2, tm, s8, LANES), F32), pltpu.VMEM((2, tm, s8, LANES), F32),
                        pltpu.VMEM((d, de), BF16), pltpu.VMEM((d, de), BF16), pltpu.VMEM((de, d), BF16),
                        pltpu.SemaphoreType.DMA((2,)), pltpu.SemaphoreType.DMA((2,))],
    )
    return pl.pallas_call(
        _experts_kernel,
        grid_spec=grid_spec,
        out_shape=jax.ShapeDtypeStruct((TOP_K * n + 2 * tm, s8, LANES), F32),
        compiler_params=_cparams("arbitrary"),
        name="experts",
    )(tile_expert, dst, n_used, h, wg, wu, wd)


def _combine_kernel(x_ref, y0_ref, y1_ref, sel_ref, gate_ref, gf_ref, out_ref, *, final_norm):
    sel = sel_ref[...]
    moe = sel[:, TOP_K:TOP_K + 1] * _from_tile_rows(y0_ref) + sel[:, TOP_K + 1:TOP_K + 2] * _from_tile_rows(y1_ref)
    y = x_ref[...] + gate_ref[...] * moe
    if final_norm:
        y = y * lax.rsqrt(jnp.mean(y * y, axis=-1, keepdims=True) + NORM_EPS) * gf_ref[...]
    out_ref[...] = y


def _combine(x, y, sel, row0, n_all, mod, layer, g_final, final_norm):
    nb, r, d = x.shape
    tm = _largest_tile(r, 128, 8)
    assert row0 % tm == 0 and n_all % tm == 0
    xspec = pl.BlockSpec((None, tm, d), lambda b, i: (b, i, 0))
    blk = lambda k: lambda b, i: (k * n_all + row0) // tm + b * (r // tm) + i
    yspec = lambda k: pl.BlockSpec((tm,) + y.shape[1:], lambda b, i: (blk(k)(b, i), 0, 0))
    return pl.pallas_call(
        functools.partial(_combine_kernel, final_norm=final_norm),
        grid=(nb, r // tm),
        in_specs=[xspec, yspec(0), yspec(1), pl.BlockSpec((tm, ROUTER_LANES), lambda b, i: (blk(0)(b, i), 0)),
                  _mod_spec(mod, layer, 5, tm, d),
                  pl.BlockSpec((1, d), lambda b, i: (0, 0))],
        out_specs=xspec,
        out_shape=jax.ShapeDtypeStruct((nb, r, d), F32),
        compiler_params=_cparams("parallel", "parallel"),
        name="combine",
    )(x, y, y, sel, mod, g_final)


def _kv_rows(k, v, heads, hd):
    return jnp.stack([k, v], axis=-2).reshape(k.shape[:-1] + (2, heads, hd))


def kernel(x_prompt, x_sample, cache_a0_kv, cache_a1_kv, cache_a2_kv, cache_b_kv, cache_c_kv, page_table,
           c_prompt, c_sample, rel_table, w_ada, b_ada, g_mix, g_ffn, g_final, w_in_a, w_out_a, w_in_b, sinks_b,
           w_out_b, w_in_c, lambda_c, subln_c, w_out_c, w_grp, b_grp, w_rt, b_rt, w_gate, w_up, w_down):
    depth, d = g_mix.shape
    bp, sp, _ = x_prompt.shape
    bs, ts, _ = x_sample.shape
    assert ts == 1
    a_caches_t = [c.transpose(0, 1, 3, 4, 5, 2) for c in (cache_a0_kv, cache_a1_kv, cache_a2_kv)]
    cache_b_t = cache_b_kv.transpose(0, 1, 3, 4, 5, 2)
    table = rel_table.astype(F32)

    m_all = _ada_all(jnp.concatenate([c_prompt, c_sample], axis=0), w_ada, b_ada)
    mod_p = m_all[:, :bp].reshape(depth, bp, 1, 6 * d)
    mod_s = m_all[:, bp:].reshape(depth, 1, bs, 6 * d)
    g_mix3 = g_mix.reshape(depth, 1, d)
    g_ffn3 = g_ffn.reshape(depth, 1, d)
    g_fin = g_final.reshape(1, d)

    w_in = [w_in_a.astype(BF16), w_in_b.astype(BF16), w_in_c.astype(BF16)]
    w_out = [w_out_a.astype(BF16), w_out_b.astype(BF16), w_out_c.astype(BF16)]
    wr = jnp.concatenate([w_grp, w_rt.transpose(0, 2, 1, 3).reshape(depth, d, N_EXPERTS)], axis=-1)
    wr = jnp.pad(wr, ((0, 0), (0, 0), (0, ROUTER_LANES - wr.shape[-1])))
    br = jnp.concatenate([b_grp, b_rt.reshape(depth, N_EXPERTS)], axis=-1)
    br = jnp.pad(br, ((0, 0), (0, ROUTER_LANES - br.shape[-1]))).reshape(depth, 1, ROUTER_LANES)

    a_bias = [_band_bias(table, r, jnp.arange(A_HEADS)) for r in A_DILATIONS]
    wa = A_HEADS * A_HD

    xp = x_prompt
    xs = x_sample.reshape(1, bs, d)
    a_p = [[] for _ in range(A_GROUPS)]
    a_new = [None] * A_GROUPS
    b_p, b_s, c_p, c_s = [], [], [], []
    for i in range(depth):
        kind, li = i % N_MIXERS, i // N_MIXERS
        qkv_p = _norm_mod_matmul(xp, g_mix3, mod_p, i, w_in[kind][li])
        qkv_s = _norm_mod_matmul(xs, g_mix3, mod_s, i, w_in[kind][li])[0]
        if kind == 0:
            res_p = [_a_prompt_group(qkv_p, g, a_bias[g]) for g in range(A_GROUPS)]
            res_s = [_a_sample_group(qkv_s, a_caches_t[g], a_new[g], li, g, table) for g in range(A_GROUPS)]
            a_new = [r[2] for r in res_s]
            xp = _proj_res([r[0] for r in res_p], [r[1] for r in res_p], w_out[0][li], xp, mod_p, i)
            xs = _proj_res([r[0][None] for r in res_s], [r[1][None] for r in res_s], w_out[0][li], xs, mod_s, i)
            for g, L in enumerate(A_WINDOWS):
                lp = min(L, sp)
                kc, vc = (A_GROUPS + g) * wa, (2 * A_GROUPS + g) * wa
                a_p[g].append(_kv_rows(qkv_p[:, sp - lp:, kc:kc + wa], qkv_p[:, sp - lp:, vc:vc + wa], A_HEADS, A_HD))
        elif kind == 1:
            o_p = _b_prompt(qkv_p, sinks_b[li], table)
            o_s = _b_sample(qkv_s, cache_b_t, li, sinks_b[li], table)
            xp = _proj_res([o_p], [], w_out[1][li], xp, mod_p, i)
            xs = _proj_res([o_s[None]], [], w_out[1][li], xs, mod_s, i)
            nq, nk = B_HEADS * B_HD, B_KV_HEADS * B_HD
            lp = min(B_WINDOW, sp)
            b_p.append(_kv_rows(qkv_p[:, sp - lp:, nq:nq + nk], qkv_p[:, sp - lp:, nq + nk:], B_KV_HEADS, B_HD))
            b_s.append(_kv_rows(qkv_s[:, nq:nq + nk], qkv_s[:, nq + nk:], B_KV_HEADS, B_HD))
        else:
            lam_init = 0.8 - 0.6 * math.exp(-0.3 * i)
            o_p = _c_prompt(qkv_p, lambda_c[li], subln_c[li], table, lam_init)
            o_s = _c_sample(qkv_s, cache_c_kv, li, page_table, lambda_c[li], subln_c[li], table, lam_init)
            xp = _proj_res([o_p], [], w_out[2][li], xp, mod_p, i)
            xs = _proj_res([o_s[None]], [], w_out[2][li], xs, mod_s, i)
            nq, nk = C_HEADS * C_VD, C_KV_HEADS * C_VD
            c_p.append(_kv_rows(qkv_p[:, :, nq:nq + nk], qkv_p[:, :, nq + nk:], C_KV_HEADS, C_VD))
            c_s.append(_kv_rows(qkv_s[:, None, nq:nq + nk], qkv_s[:, None, nq + nk:], C_KV_HEADS, C_VD))
        last = i == depth - 1
        n_p, n_all = bp * sp, bp * sp + bs
        h_all, sel = _router(xp, xs, g_ffn3, mod_p, mod_s, i, wr, br)
        y = _experts(h_all, sel, w_gate, w_up, w_down, i)
        xp = _combine(xp, y, sel, 0, n_all, mod_p, i, g_fin, last)
        xs = _combine(xs, y, sel, n_p, n_all, mod_s, i, g_fin, last)

    def shifted(cache, new_rows):
        return jnp.concatenate([cache[:, :, 1:], jnp.stack(new_rows)[:, :, None]], axis=2)

    a_out = [c.transpose(0, 1, 5, 2, 3, 4) for c in a_new]
    return (xp, xs.reshape(bs, ts, d),
            jnp.stack(a_p[0]), a_out[0],
            jnp.stack(a_p[1]), a_out[1],
            jnp.stack(a_p[2]), a_out[2],
            jnp.stack(b_p), shifted(cache_b_kv, b_s),
            jnp.stack(c_p), jnp.stack(c_s))
```

```python
import functools
import math

import jax
import jax.numpy as jnp
from jax import lax
from jax.experimental import pallas as pl
from jax.experimental.pallas import tpu as pltpu

F32 = jnp.float32
BF16 = jnp.bfloat16

N_MIXERS = 3
A_WINDOWS = (128, 512, 2048)
A_DILATIONS = (1, 4, 16)
A_GROUPS = 3
A_HEADS = 8
A_HD = 64
B_WINDOW = 128
B_HEADS = 16
B_KV_HEADS = 2
B_GROUP = B_HEADS // B_KV_HEADS
B_HD = 64
C_HEADS = 8
C_KV_HEADS = 4
C_GROUP = C_HEADS // C_KV_HEADS
C_HD = 64
C_VD = 2 * C_HD
NUM_BUCKETS = 32
MAX_DISTANCE = 2048
N_GROUPS = 4
EXPERTS_PER_GROUP = 8
N_EXPERTS = N_GROUPS * EXPERTS_PER_GROUP
PAGE_SIZE = 128
QBLOCK = 128
NORM_EPS = 1e-6
NEG = -1e30

LANES = 128
ROUTER_LANES = LANES
VMEM_LIMIT = 56 * 1024 * 1024


def _cparams(*sem):
    return pltpu.CompilerParams(dimension_semantics=sem, vmem_limit_bytes=VMEM_LIMIT)


def _largest_tile(n, cap, mult):
    best = None
    for t in range(mult, min(n, cap) + 1, mult):
        if n % t == 0:
            best = t
    assert best is not None, (n, cap, mult)
    return best


def _t5_bucket(dist):
    max_exact = NUM_BUCKETS // 2
    d = jnp.maximum(dist, 0)
    ratio = jnp.log(jnp.maximum(d, 1).astype(F32) / max_exact) / math.log(MAX_DISTANCE / max_exact)
    large = jnp.minimum(max_exact + (ratio * (NUM_BUCKETS - max_exact)).astype(jnp.int32), NUM_BUCKETS - 1)
    return jnp.where(d < max_exact, d, large)


def _rel_bias(table, dist):
    return table.astype(F32)[_t5_bucket(dist)]


def _toeplitz(w, rows, cols):
    m = w.shape[-1]
    assert m >= rows + cols - 1
    flat = jnp.tile(w, (1,) * (w.ndim - 1) + (rows,))[..., :rows * (m - 1)]
    return flat.reshape(w.shape[:-1] + (rows, m - 1))[..., :cols]


def _band_bias(table, dil, head_cols):
    m = 3 * QBLOCK
    u = jnp.arange(m, dtype=jnp.int32)
    u = jnp.where(u < 2 * QBLOCK, u, u - m)
    dist = QBLOCK - u
    ok = (dist >= 0) & (dist <= QBLOCK)
    w = jnp.where(ok[None], _rel_bias(table, dil * dist)[:, head_cols].T, NEG)
    return _toeplitz(w, QBLOCK, 2 * QBLOCK)


def _ada_kernel(c_ref, w_ref, b_ref, o_ref):
    c = c_ref[...]
    a = c * jax.nn.sigmoid(c)
    o_ref[...] = jnp.dot(a.astype(BF16), w_ref[...].astype(BF16), preferred_element_type=F32) + b_ref[...]


def _ada_all(c_all, w_ada, b_ada):
    depth, d, f = w_ada.shape
    n = c_all.shape[0]
    tn = _largest_tile(f, 1024, LANES)
    return pl.pallas_call(
        _ada_kernel,
        grid=(depth, f // tn),
        in_specs=[pl.BlockSpec((n, d), lambda i, j: (0, 0)),
                  pl.BlockSpec((None, d, tn), lambda i, j: (i, 0, j)),
                  pl.BlockSpec((None, 1, tn), lambda i, j: (i, 0, j))],
        out_specs=pl.BlockSpec((None, n, tn), lambda i, j: (i, 0, j)),
        out_shape=jax.ShapeDtypeStruct((depth, n, f), F32),
        compiler_params=_cparams("parallel", "parallel"),
        name="ada",
    )(c_all, w_ada, b_ada.reshape(depth, 1, f))


def _mod_spec(mod, layer, chunk, tm, width):
    per_row = mod.shape[2] != 1
    rows = tm if per_row else 1
    per = mod.shape[3] // 6 // width
    return pl.BlockSpec((None, None, rows, width),
                        lambda b, i, *_: (layer, b, i if per_row else 0, chunk * per))


def _norm_mod(x, g, shift, scale):
    y = x * lax.rsqrt(jnp.mean(x * x, axis=-1, keepdims=True) + NORM_EPS) * g
    return y * (1.0 + scale) + shift


def _nmm_kernel(x_ref, g_ref, sh_ref, sc_ref, w_ref, o_ref, h_ref):
    @pl.when(pl.program_id(2) == 0)
    def _():
        h_ref[...] = _norm_mod(x_ref[...], g_ref[...], sh_ref[...], sc_ref[...]).astype(h_ref.dtype)

    o_ref[...] = jnp.dot(h_ref[...], w_ref[...], preferred_element_type=F32)


def _norm_mod_matmul(x, g, mod, layer, w):
    nb, r, d = x.shape
    f = w.shape[1]
    tm = _largest_tile(r, 512, 8)
    tn = _largest_tile(f, 1536, LANES)
    return pl.pallas_call(
        _nmm_kernel,
        grid=(nb, r // tm, f // tn),
        in_specs=[pl.BlockSpec((None, tm, d), lambda b, i, j: (b, i, 0)),
                  pl.BlockSpec((None, 1, d), lambda b, i, j: (layer, 0, 0)),
                  _mod_spec(mod, layer, 0, tm, d),
                  _mod_spec(mod, layer, 1, tm, d),
                  pl.BlockSpec((d, tn), lambda b, i, j: (0, j))],
        out_specs=pl.BlockSpec((None, tm, tn), lambda b, i, j: (b, i, j)),
        out_shape=jax.ShapeDtypeStruct((nb, r, f), F32),
        scratch_shapes=[pltpu.VMEM((tm, d), BF16)],
        compiler_params=_cparams("parallel", "parallel", "arbitrary"),
        name="norm_qkv",
    )(x, g, mod, mod, w)


def _proj_res_kernel(o_ref, w_ref, x_ref, gate_ref, out_ref):
    y = jnp.dot(o_ref[...].astype(BF16), w_ref[...], preferred_element_type=F32)
    out_ref[...] = x_ref[...] + gate_ref[...] * y


def _merge_proj_res_kernel(o0, o1, o2, l0, l1, l2, w_ref, x_ref, gate_ref, out_ref):
    a0, a1, a2 = l0[...], l1[...], l2[...]
    m = jnp.maximum(jnp.maximum(a0, a1), a2)
    e0, e1, e2 = jnp.exp(a0 - m), jnp.exp(a1 - m), jnp.exp(a2 - m)
    o = (e0 * o0[...] + e1 * o1[...] + e2 * o2[...]) / (e0 + e1 + e2)
    y = jnp.dot(o.astype(BF16), w_ref[...], preferred_element_type=F32)
    out_ref[...] = x_ref[...] + gate_ref[...] * y


def _proj_res(os_, lses, w, x, mod, layer):
    nb, r, d = x.shape
    k = w.shape[0]
    tm = _largest_tile(r, 512, 8)
    ospec = pl.BlockSpec((None, tm, k), lambda b, i: (b, i, 0))
    xspec = pl.BlockSpec((None, tm, d), lambda b, i: (b, i, 0))
    ins = list(os_) + list(lses)
    kern = _merge_proj_res_kernel if lses else _proj_res_kernel
    return pl.pallas_call(
        kern,
        grid=(nb, r // tm),
        in_specs=[ospec] * len(ins) + [pl.BlockSpec((k, d), lambda b, i: (0, 0)), xspec,
                                       _mod_spec(mod, layer, 2, tm, d)],
        out_specs=xspec,
        out_shape=jax.ShapeDtypeStruct((nb, r, d), F32),
        compiler_params=_cparams("parallel", "parallel"),
        name="proj_res",
    )(*ins, w, x, mod)


def _a_prompt_kernel(q_ref, kp_ref, kc_ref, vp_ref, vc_ref, bias_ref, o_ref, l_ref):
    bi = pl.program_id(2)
    scale = A_HD ** -0.5
    q = q_ref[...].astype(BF16)
    k = jnp.concatenate([kp_ref[...], kc_ref[...]], axis=0).astype(BF16)
    v = jnp.concatenate([vp_ref[...], vc_ref[...]], axis=0).astype(BF16)
    col = lax.broadcasted_iota(jnp.int32, (QBLOCK, 2 * QBLOCK), 1)
    keep = (col >= QBLOCK) | (bi > 0)
    outs, lses = [], []
    for h in range(A_HEADS):
        sl = slice(h * A_HD, (h + 1) * A_HD)
        s = lax.dot_general(q[:, sl], k[:, sl], (((1,), (1,)), ((), ())), preferred_element_type=F32)
        s = jnp.where(keep, s * scale + bias_ref[h], NEG)
        m = jnp.max(s, axis=-1, keepdims=True)
        p = jnp.exp(s - m)
        l = jnp.sum(p, axis=-1, keepdims=True)
        outs.append(jnp.dot(p.astype(BF16), v[:, sl], preferred_element_type=F32) / l)
        lses.append(jnp.broadcast_to(m + jnp.log(l), (QBLOCK, A_HD)))
    o_ref[...] = jnp.concatenate(outs, axis=1)
    l_ref[...] = jnp.concatenate(lses, axis=1)


def _a_dilated_kernel(q_ref, kp_ref, kc_ref, vp_ref, vc_ref, bias_ref, o_ref, l_ref, *, dil):
    bi = pl.program_id(1)
    scale = A_HD ** -0.5
    col = lax.broadcasted_iota(jnp.int32, (QBLOCK, 2 * QBLOCK), 1)
    keep = (col >= QBLOCK) | (bi > 0)

    def one_class(c, carry):
        rows = pl.ds(c, QBLOCK, stride=dil)
        q = (q_ref[rows, :] * scale).astype(BF16)
        k = jnp.concatenate([kp_ref[rows, :], kc_ref[rows, :]], axis=0).astype(BF16)
        v = jnp.concatenate([vp_ref[rows, :], vc_ref[rows, :]], axis=0).astype(BF16)
        outs, lses = [], []
        for h in range(LANES // A_HD):
            sl = slice(h * A_HD, (h + 1) * A_HD)
            s = lax.dot_general(q[:, sl], k[:, sl], (((1,), (1,)), ((), ())), preferred_element_type=F32)
            s = jnp.where(keep, s + bias_ref[h], NEG)
            m = jnp.max(s, axis=-1, keepdims=True)
            p = jnp.exp(s - m)
            l = jnp.sum(p, axis=-1, keepdims=True)
            outs.append(jnp.dot(p.astype(BF16), v[:, sl], preferred_element_type=F32) / l)
            lses.append(jnp.broadcast_to(m + jnp.log(l), (QBLOCK, A_HD)))
        o_ref[rows, :] = jnp.concatenate(outs, axis=1)
        l_ref[rows, :] = jnp.concatenate(lses, axis=1)
        return carry

    lax.fori_loop(0, dil, one_class, 0, unroll=4)


def _a_dilated_group(qkv, g, bias):
    b, t, f = qkv.shape
    dil = A_DILATIONS[g]
    w = A_HEADS * A_HD
    chunk = QBLOCK * dil
    hpb = w // LANES
    hpl = LANES // A_HD

    def spec(sec, prev):
        return pl.BlockSpec((None, chunk, LANES),
                            lambda bb, i, hp: (bb, jnp.maximum(i - 1, 0) if prev else i,
                                               (sec * A_GROUPS + g) * hpb + hp))

    ospec = pl.BlockSpec((None, chunk, LANES), lambda bb, i, hp: (bb, i, hp))
    oshape = jax.ShapeDtypeStruct((b, t, w), F32)
    return pl.pallas_call(
        functools.partial(_a_dilated_kernel, dil=dil),
        grid=(b, t // chunk, hpb),
        in_specs=[spec(0, False), spec(1, True), spec(1, False), spec(2, True), spec(2, False),
                  pl.BlockSpec((hpl, QBLOCK, 2 * QBLOCK), lambda bb, i, hp: (hp, 0, 0))],
        out_specs=[ospec, ospec],
        out_shape=[oshape, oshape],
        compiler_params=_cparams("parallel", "parallel", "parallel"),
        name=f"a_prompt_g{g}",
    )(qkv, qkv, qkv, qkv, qkv, bias)


def _a_prompt_group(qkv, g, bias):
    if A_DILATIONS[g] > 1:
        return _a_dilated_group(qkv, g, bias)
    b, t, f = qkv.shape
    dil = A_DILATIONS[g]
    lr = t // dil
    nq = lr // QBLOCK
    w = A_HEADS * A_HD
    per = f // w
    view = qkv.reshape(b, lr, dil * f)

    def spec(sec, prev):
        return pl.BlockSpec((None, QBLOCK, w),
                            lambda bb, c, i: (bb, jnp.maximum(i - 1, 0) if prev else i, c * per + sec * A_GROUPS + g))

    ospec = pl.BlockSpec((None, QBLOCK, w), lambda bb, c, i: (bb, i, c))
    oshape = jax.ShapeDtypeStruct((b, lr, dil * w), F32)
    o, l = pl.pallas_call(
        _a_prompt_kernel,
        grid=(b, dil, nq),
        in_specs=[spec(0, False), spec(1, True), spec(1, False), spec(2, True), spec(2, False),
                  pl.BlockSpec((A_HEADS, QBLOCK, 2 * QBLOCK), lambda bb, c, i: (0, 0, 0))],
        out_specs=[ospec, ospec],
        out_shape=[oshape, oshape],
        compiler_params=_cparams("parallel", "parallel", "parallel"),
        name=f"a_prompt_g{g}",
    )(view, view, view, view, view, bias)
    return o.reshape(b, t, w), l.reshape(b, t, w)


def _decode_head(kt, vt, q, kn, vn, bias, bias_new, scale, sink=None):
    s = jnp.sum(kt * q, axis=1, keepdims=True) * scale + bias
    s_new = jnp.sum(kn * q, axis=1, keepdims=True) * scale + bias_new
    m = jnp.maximum(jnp.max(s, axis=-1, keepdims=True), s_new)
    if sink is not None:
        m = jnp.maximum(m, sink)
    p = jnp.exp(s - m)
    pn = jnp.exp(s_new - m)
    l = jnp.sum(p, axis=-1, keepdims=True) + pn
    if sink is not None:
        l = l + jnp.exp(sink - m)
    o = (jnp.sum(vt * p, axis=-1, keepdims=True) + pn * vn) / l
    return o, m + jnp.log(l)


def _a_sample_kernel(q_ref, kvn_ref, c_ref, bias_ref, bn_ref, *rest):
    o_ref, l_ref, out_ref = rest[-3:]
    L = c_ref.shape[-1]
    scale = A_HD ** -0.5
    last = lax.broadcasted_iota(jnp.int32, (1, 1, L), 2) == L - 1
    for h in range(A_HEADS):
        kt, vt = c_ref[:, 0, h], c_ref[:, 1, h]
        kn, vn = kvn_ref[:, 0, h], kvn_ref[:, 1, h]
        o, lse = _decode_head(kt, vt, q_ref[:, h], kn, vn, bias_ref[h], bn_ref[h], scale)
        o_ref[:, h] = o
        l_ref[:, h] = lse
        out_ref[:, 0, h] = jnp.where(last, kn, pltpu.roll(kt, L - 1, 2))
        out_ref[:, 1, h] = jnp.where(last, vn, pltpu.roll(vt, L - 1, 2))


def _a_sample_group(qkv, cache_t, prev_out, li, g, table):
    n, f = qkv.shape
    dil = A_DILATIONS[g]
    n_lay, nb, two, hh, hd, L = cache_t.shape
    assert (nb, two, hh, hd) == (n, 2, A_HEADS, A_HD) and L == A_WINDOWS[g]
    w = A_HEADS * A_HD
    col = lambda sec: qkv[:, (sec * A_GROUPS + g) * w:(sec * A_GROUPS + g + 1) * w]
    q = col(0).reshape(n, A_HEADS, A_HD, 1)
    kvn = jnp.stack([col(1), col(2)], axis=1).reshape(n, 2, A_HEADS, A_HD, 1)
    dist = L - jnp.arange(L, dtype=jnp.int32)
    bias = jnp.where((dist % dil == 0)[None], _rel_bias(table, dist).T, NEG).reshape(A_HEADS, 1, L)
    bn = _rel_bias(table, jnp.zeros((1,), jnp.int32)).T.reshape(A_HEADS, 1, 1)
    bt = max(1, (4 * 1024 * 1024) // (2 * w * L * 4))
    cspec = pl.BlockSpec((None, bt, 2, A_HEADS, A_HD, L), lambda b: (li, b, 0, 0, 0, 0))
    in_specs = [pl.BlockSpec((bt, A_HEADS, A_HD, 1), lambda b: (b, 0, 0, 0)),
                pl.BlockSpec((bt, 2, A_HEADS, A_HD, 1), lambda b: (b, 0, 0, 0, 0)),
                cspec,
                pl.BlockSpec((A_HEADS, 1, L), lambda b: (0, 0, 0)),
                pl.BlockSpec((A_HEADS, 1, 1), lambda b: (0, 0, 0))]
    args = [q, kvn, cache_t, bias, bn]
    aliases = {}
    if prev_out is not None:
        in_specs.append(pl.BlockSpec(memory_space=pl.ANY))
        args.append(prev_out)
        aliases = {len(args) - 1: 2}
    o, lse, out = pl.pallas_call(
        _a_sample_kernel,
        grid=(n // bt,),
        in_specs=in_specs,
        out_specs=[pl.BlockSpec((bt, A_HEADS, A_HD, 1), lambda b: (b, 0, 0, 0)),
                   pl.BlockSpec((bt, A_HEADS, 1, 1), lambda b: (b, 0, 0, 0)),
                   cspec],
        out_shape=[jax.ShapeDtypeStruct((n, A_HEADS, A_HD, 1), F32),
                   jax.ShapeDtypeStruct((n, A_HEADS, 1, 1), F32),
                   jax.ShapeDtypeStruct(cache_t.shape, F32)],
        input_output_aliases=aliases,
        compiler_params=_cparams("parallel"),
        name=f"a_sample_g{g}",
    )(*args)
    lse = jnp.broadcast_to(lse.reshape(n, A_HEADS, 1), (n, A_HEADS, A_HD)).reshape(n, w)
    return o.reshape(n, w), lse, out


def _head_mask(rows, width, hd, lane_head_of_row):
    lane = lax.broadcasted_iota(jnp.int32, (rows, width), 1) // hd
    row = lax.broadcasted_iota(jnp.int32, (rows, width), 0)
    return (lane == lane_head_of_row(row)).astype(F32)


def _b_prompt_kernel(q_ref, kp_ref, kc_ref, vp_ref, vc_ref, bias_ref, sink_ref, o_ref):
    bi = pl.program_id(1)
    scale = B_HD ** -0.5
    q = q_ref[...].astype(BF16)
    k = jnp.concatenate([kp_ref[...], kc_ref[...]], axis=0).astype(BF16)
    v = jnp.concatenate([vp_ref[...], vc_ref[...]], axis=0).astype(BF16)
    col = lax.broadcasted_iota(jnp.int32, (B_GROUP * QBLOCK, 2 * QBLOCK), 1)
    keep = (col >= QBLOCK) | (bi > 0)
    outs = []
    for kh in range(B_KV_HEADS):
        qs = jnp.concatenate([q[:, (kh * B_GROUP + g) * B_HD:(kh * B_GROUP + g + 1) * B_HD]
                              for g in range(B_GROUP)], axis=0)
        sl = slice(kh * B_HD, (kh + 1) * B_HD)
        s = lax.dot_general(qs, k[:, sl], (((1,), (1,)), ((), ())), preferred_element_type=F32)
        s = jnp.where(keep, s * scale + bias_ref[kh], NEG)
        sink = sink_ref[kh]
        m = jnp.maximum(jnp.max(s, axis=-1, keepdims=True), sink)
        p = jnp.exp(s - m)
        den = jnp.sum(p, axis=-1, keepdims=True) + jnp.exp(sink - m)
        o = jnp.dot(p.astype(BF16), v[:, sl], preferred_element_type=F32) / den
        outs += [o[g * QBLOCK:(g + 1) * QBLOCK] for g in range(B_GROUP)]
    o_ref[...] = jnp.concatenate(outs, axis=1)


def _b_prompt(qkv, sinks, table):
    b, t, f = qkv.shape
    nq = t // QBLOCK
    wq = B_HEADS * B_HD
    wk = B_KV_HEADS * B_HD
    head_cols = jnp.arange(B_HEADS) // (B_HEADS // table.shape[1])
    bias = _band_bias(table, 1, head_cols).reshape(B_KV_HEADS, B_GROUP * QBLOCK, 2 * QBLOCK)
    sink = jnp.broadcast_to(sinks.astype(F32).reshape(B_KV_HEADS, B_GROUP, 1, 1),
                            (B_KV_HEADS, B_GROUP, QBLOCK, 1)).reshape(B_KV_HEADS, B_GROUP * QBLOCK, 1)

    def kvspec(colblock, prev):
        return pl.BlockSpec((None, QBLOCK, wk), lambda bb, i: (bb, jnp.maximum(i - 1, 0) if prev else i, colblock))

    kcol = wq // wk
    return pl.pallas_call(
        _b_prompt_kernel,
        grid=(b, nq),
        in_specs=[pl.BlockSpec((None, QBLOCK, wq), lambda bb, i: (bb, i, 0)),
                  kvspec(kcol, True), kvspec(kcol, False), kvspec(kcol + 1, True), kvspec(kcol + 1, False),
                  pl.BlockSpec(bias.shape, lambda bb, i: (0, 0, 0)),
                  pl.BlockSpec(sink.shape, lambda bb, i: (0, 0, 0))],
        out_specs=pl.BlockSpec((None, QBLOCK, wq), lambda bb, i: (bb, i, 0)),
        out_shape=jax.ShapeDtypeStruct((b, t, wq), F32),
        compiler_params=_cparams("parallel", "parallel"),
        name="b_prompt",
    )(qkv, qkv, qkv, qkv, qkv, bias, sink)


def _b_sample_kernel(q_ref, kvn_ref, c_ref, bias_ref, bn_ref, sink_ref, o_ref):
    scale = B_HD ** -0.5
    for hq in range(B_HEADS):
        kh = hq // B_GROUP
        o, _ = _decode_head(c_ref[:, 0, kh], c_ref[:, 1, kh], q_ref[:, hq], kvn_ref[:, 0, kh], kvn_ref[:, 1, kh],
                            bias_ref[hq], bn_ref[hq], scale, sink_ref[hq])
        o_ref[:, hq] = o


def _b_sample(qkv, cache_t, li, sinks, table):
    n, f = qkv.shape
    n_lay, nb, two, kvh, hd, L = cache_t.shape
    assert (nb, two, kvh, hd) == (n, 2, B_KV_HEADS, B_HD) and L == B_WINDOW
    wq = B_HEADS * B_HD
    wk = B_KV_HEADS * B_HD
    head_cols = jnp.arange(B_HEADS) // (B_HEADS // table.shape[1])
    dist = L - jnp.arange(L, dtype=jnp.int32)
    bias = _rel_bias(table, dist)[:, head_cols].T.reshape(B_HEADS, 1, L)
    bn = _rel_bias(table, jnp.zeros((1,), jnp.int32))[:, head_cols].T.reshape(B_HEADS, 1, 1)
    q = qkv[:, :wq].reshape(n, B_HEADS, B_HD, 1)
    kvn = jnp.stack([qkv[:, wq:wq + wk], qkv[:, wq + wk:]], axis=1).reshape(n, 2, B_KV_HEADS, B_HD, 1)
    bt = 8
    o = pl.pallas_call(
        _b_sample_kernel,
        grid=(n // bt,),
        in_specs=[pl.BlockSpec((bt, B_HEADS, B_HD, 1), lambda b: (b, 0, 0, 0)),
                  pl.BlockSpec((bt, 2, B_KV_HEADS, B_HD, 1), lambda b: (b, 0, 0, 0, 0)),
                  pl.BlockSpec((None, bt, 2, B_KV_HEADS, B_HD, L), lambda b: (li, b, 0, 0, 0, 0)),
                  pl.BlockSpec((B_HEADS, 1, L), lambda b: (0, 0, 0)),
                  pl.BlockSpec((B_HEADS, 1, 1), lambda b: (0, 0, 0)),
                  pl.BlockSpec((B_HEADS, 1, 1), lambda b: (0, 0, 0))],
        out_specs=pl.BlockSpec((bt, B_HEADS, B_HD, 1), lambda b: (b, 0, 0, 0)),
        out_shape=jax.ShapeDtypeStruct((n, B_HEADS, B_HD, 1), F32),
        compiler_params=_cparams("parallel"),
        name="b_sample",
    )(q, kvn, cache_t, bias, bn, sinks.astype(F32).reshape(B_HEADS, 1, 1))
    return o.reshape(n, wq)


def _diff_lambda(lp, lam_init):
    return (jnp.exp(jnp.sum(lp[0:1] * lp[1:2], axis=-1, keepdims=True))
            - jnp.exp(jnp.sum(lp[2:3] * lp[3:4], axis=-1, keepdims=True)) + lam_init)


def _diff_out(o1, o2, lam, subln, lam_init):
    d = o1 - lam * o2
    y = d * lax.rsqrt(jnp.mean(d * d, axis=-1, keepdims=True) + NORM_EPS) * subln
    return y * (1.0 - lam_init)


C_BLOCK = 512


def _c_prompt_kernel(qi_tab, ki_tab, q_ref, k_ref, v_ref, bias_ref, lam_ref, subln_ref, o_ref,
                     m_sc, l_sc, acc_sc, *, lam_init):
    t = pl.program_id(2)
    qi = qi_tab[t]
    ki = ki_tab[t]
    scale = C_HD ** -0.5

    @pl.when(ki == 0)
    def _():
        m_sc[...] = jnp.full(m_sc.shape, NEG, F32)
        l_sc[...] = jnp.zeros(l_sc.shape, F32)
        acc_sc[...] = jnp.zeros(acc_sc.shape, F32)

    q = (q_ref[...] * scale).astype(BF16)
    k = k_ref[...].astype(BF16)
    v = v_ref[...].astype(BF16)
    for g in range(C_GROUP):
        bias = bias_ref[g]
        for mp in range(2):
            idx = g * 2 + mp
            qs = q[:, g * C_VD + mp * C_HD:g * C_VD + (mp + 1) * C_HD]
            ks = k[:, mp * C_HD:(mp + 1) * C_HD]
            s = lax.dot_general(qs, ks, (((1,), (1,)), ((), ())), preferred_element_type=F32) + bias
            m_old = m_sc[idx]
            m_new = jnp.maximum(m_old, jnp.max(s, axis=-1, keepdims=True))
            alpha = jnp.exp(m_old - m_new)
            p = jnp.exp(s - jnp.tile(m_new, (1, s.shape[1] // LANES)))
            l_sc[idx] = alpha * l_sc[idx] + jnp.sum(p, axis=-1, keepdims=True)
            acc_sc[idx] = alpha * acc_sc[idx] + jnp.dot(p.astype(BF16), v, preferred_element_type=F32)
            m_sc[idx] = m_new

    @pl.when(ki == qi)
    def _():
        lam = _diff_lambda(lam_ref[...], lam_init)
        outs = []
        for g in range(C_GROUP):
            o1 = acc_sc[g * 2] / l_sc[g * 2]
            o2 = acc_sc[g * 2 + 1] / l_sc[g * 2 + 1]
            outs.append(_diff_out(o1, o2, lam, subln_ref[...], lam_init))
        o_ref[...] = jnp.concatenate(outs, axis=1)


def _c_prompt(qkv, lam_p, subln, table, lam_init):
    b, t, f = qkv.shape
    blk = C_BLOCK
    nblk = t // blk
    pairs = [(qi, ki) for qi in range(nblk) for ki in range(qi + 1)]
    qi_tab = jnp.asarray([p[0] for p in pairs], jnp.int32)
    ki_tab = jnp.asarray([p[1] for p in pairs], jnp.int32)
    u = jnp.arange(2 * blk, dtype=jnp.int32)
    u = jnp.where(u < blk, u, u - 2 * blk)
    dist = jnp.arange(nblk, dtype=jnp.int32)[:, None] * blk - u[None, :]
    w = jnp.where((dist >= 0)[:, None], _rel_bias(table, dist).transpose(0, 2, 1), NEG)
    bias = _toeplitz(w, blk, blk)
    wq = C_GROUP * C_VD
    qcols = C_HEADS * C_VD
    grid_spec = pltpu.PrefetchScalarGridSpec(
        num_scalar_prefetch=2,
        grid=(b, C_KV_HEADS, len(pairs)),
        in_specs=[pl.BlockSpec((None, blk, wq), lambda bb, kh, tt, qt, kt: (bb, qt[tt], kh)),
                  pl.BlockSpec((None, blk, C_VD), lambda bb, kh, tt, qt, kt: (bb, kt[tt], qcols // C_VD + kh)),
                  pl.BlockSpec((None, blk, C_VD),
                               lambda bb, kh, tt, qt, kt: (bb, kt[tt], qcols // C_VD + C_KV_HEADS + kh)),
                  pl.BlockSpec((None, C_GROUP, blk, blk), lambda bb, kh, tt, qt, kt: (qt[tt] - kt[tt], kh, 0, 0)),
                  pl.BlockSpec((4, C_HD), lambda bb, kh, tt, qt, kt: (0, 0)),
                  pl.BlockSpec((1, C_VD), lambda bb, kh, tt, qt, kt: (0, 0))],
        out_specs=pl.BlockSpec((None, blk, wq), lambda bb, kh, tt, qt, kt: (bb, qt[tt], kh)),
        scratch_shapes=[pltpu.VMEM((2 * C_GROUP, blk, LANES), F32), pltpu.VMEM((2 * C_GROUP, blk, LANES), F32),
                        pltpu.VMEM((2 * C_GROUP, blk, C_VD), F32)],
    )
    return pl.pallas_call(
        functools.partial(_c_prompt_kernel, lam_init=lam_init),
        grid_spec=grid_spec,
        out_shape=jax.ShapeDtypeStruct((b, t, qcols), F32),
        compiler_params=_cparams("parallel", "parallel", "arbitrary"),
        name="c_prompt",
    )(qi_tab, ki_tab, qkv, qkv, qkv, bias, lam_p.astype(F32), subln.astype(F32).reshape(1, C_VD))


C_PAGES_PER_STEP = 8
C_MAPS = 2 * C_HEADS
C_ROWS = 2 * C_KV_HEADS


def _c_sample_kernel(pt_ref, q_ref, kvn_ref, lam_ref, subln_ref, bias_ref, bn_ref, *rest, lam_init):
    pages = rest[:C_PAGES_PER_STEP]
    o_ref, m_sc, l_sc, acc_sc = rest[C_PAGES_PER_STEP:]
    j = pl.program_id(1)
    scale = C_HD ** -0.5
    hm = _head_mask(C_MAPS, C_VD, C_HD, lambda r: r // C_HEADS)
    qe = jnp.concatenate([q_ref[...]] * 2, axis=-1) * hm
    qb = (qe * scale).astype(BF16)
    row_kh = (lax.broadcasted_iota(jnp.int32, (C_MAPS, 1), 0) % C_HEADS) // C_GROUP

    @pl.when(j == 0)
    def _():
        kvn = kvn_ref[...]
        s_new = jnp.zeros((C_MAPS, 1), F32)
        v_new = jnp.zeros((C_MAPS, C_VD), F32)
        for kh in range(C_KV_HEADS):
            s_kh = jnp.sum(qe * kvn[kh:kh + 1], axis=-1, keepdims=True)
            s_new = jnp.where(row_kh == kh, s_kh, s_new)
            v_new = jnp.where(row_kh == kh, kvn[C_KV_HEADS + kh:C_KV_HEADS + kh + 1], v_new)
        m_sc[...] = s_new * scale + bn_ref[...]
        l_sc[...] = jnp.ones(l_sc.shape, F32)
        acc_sc[...] = v_new

    rows = [pages[pi][...].astype(BF16) for pi in range(C_PAGES_PER_STEP)]
    ss = [lax.dot_general(qb, rows[pi], (((1,), (1,)), ((), ())), preferred_element_type=F32) + bias_ref[pi]
          for pi in range(C_PAGES_PER_STEP)]
    m = m_sc[...]
    m_new = m
    for s in ss:
        m_new = jnp.maximum(m_new, jnp.max(s, axis=-1, keepdims=True))
    alpha = jnp.exp(m - m_new)
    l = alpha * l_sc[...]
    acc = alpha * acc_sc[...]
    for pi in range(C_PAGES_PER_STEP):
        p = jnp.exp(ss[pi] - m_new)
        l = l + jnp.sum(p, axis=-1, keepdims=True)
        pv = pltpu.roll(p, C_KV_HEADS, 1)
        acc = acc + jnp.dot(pv.astype(BF16), rows[pi], preferred_element_type=F32)
    m_sc[...] = m_new
    l_sc[...] = l
    acc_sc[...] = acc

    @pl.when(j == pl.num_programs(1) - 1)
    def _():
        o = acc / l
        lam = _diff_lambda(lam_ref[...], lam_init)
        o_ref[...] = _diff_out(o[:C_HEADS], o[C_HEADS:], lam, subln_ref[...], lam_init)


def _c_sample(qkv, cache, ci, page_table, lam_p, subln, table, lam_init):
    n, f = qkv.shape
    n_lay, n_phys, page, two, kvh, vd = cache.shape
    assert (page, two, kvh, vd) == (PAGE_SIZE, 2, C_KV_HEADS, C_VD)
    n_pages = page_table.shape[1]
    pps = C_PAGES_PER_STEP
    assert n_pages % pps == 0
    qcols = C_HEADS * C_VD
    prow = page * C_ROWS
    view = cache.reshape(n_lay, n_phys, prow, C_VD)
    q16 = qkv[:, :qcols].reshape(n, C_HEADS, 2, C_HD).transpose(0, 2, 1, 3).reshape(n, C_MAPS, C_HD)
    kvn = qkv[:, qcols:].reshape(n, C_ROWS, C_VD)
    n_keys = n_pages * page
    b8 = _rel_bias(table, n_keys - jnp.arange(n_keys, dtype=jnp.int32)).T
    b16 = jnp.concatenate([b8, b8], axis=0)
    row_kh = (jnp.arange(C_MAPS) % C_HEADS) // C_GROUP
    own = row_kh[:, None, None] == jnp.arange(C_ROWS)[None, None, :]
    bias = jnp.where(own, b16[:, :, None], NEG).reshape(C_MAPS, n_pages, prow).transpose(1, 0, 2)
    bn = _rel_bias(table, jnp.zeros((1,), jnp.int32)).T
    bn = jnp.concatenate([bn, bn], axis=0)
    pt_flat = page_table.reshape(-1).astype(jnp.int32)

    def page_spec(pi):
        return pl.BlockSpec((None, None, prow, C_VD),
                            lambda b, j, pt: (ci, pt[b * n_pages + j * pps + pi], 0, 0))

    grid_spec = pltpu.PrefetchScalarGridSpec(
        num_scalar_prefetch=1,
        grid=(n, n_pages // pps),
        in_specs=[pl.BlockSpec((None, C_MAPS, C_HD), lambda b, j, pt: (b, 0, 0)),
                  pl.BlockSpec((None, C_ROWS, C_VD), lambda b, j, pt: (b, 0, 0)),
                  pl.BlockSpec((4, C_HD), lambda b, j, pt: (0, 0)),
                  pl.BlockSpec((1, C_VD), lambda b, j, pt: (0, 0)),
                  pl.BlockSpec((pps, C_MAPS, prow), lambda b, j, pt: (j, 0, 0)),
                  pl.BlockSpec((C_MAPS, 1), lambda b, j, pt: (0, 0))] + [page_spec(pi) for pi in range(pps)],
        out_specs=pl.BlockSpec((None, C_HEADS, C_VD), lambda b, j, pt: (b, 0, 0)),
        scratch_shapes=[pltpu.VMEM((C_MAPS, 1), F32), pltpu.VMEM((C_MAPS, 1), F32), pltpu.VMEM((C_MAPS, C_VD), F32)],
    )
    o = pl.pallas_call(
        functools.partial(_c_sample_kernel, lam_init=lam_init),
        grid_spec=grid_spec,
        out_shape=jax.ShapeDtypeStruct((n, C_HEADS, C_VD), F32),
        compiler_params=_cparams("parallel", "arbitrary"),
        name="c_sample",
    )(pt_flat, q16, kvn, lam_p.astype(F32), subln.astype(F32).reshape(1, C_VD), bias, bn, *([view] * pps))
    return o.reshape(n, qcols)


def _to_tile_rows(ref, x):
    for c in range(ref.shape[1]):
        ref[:, c, :] = x[:, c * LANES:(c + 1) * LANES]


def _from_tile_rows(ref):
    return jnp.concatenate([ref[:, c, :] for c in range(ref.shape[1])], axis=1)


def _route(x, g, shift, scale, wr, br, h_ref, sel_ref):
    h = _norm_mod(x, g, shift, scale)
    _to_tile_rows(h_ref, h)
    logits = jnp.dot(h, wr, preferred_element_type=F32, precision=lax.Precision.HIGHEST) + br
    lane = lax.broadcasted_iota(jnp.int32, logits.shape, 1).astype(F32)
    first = lambda hit: jnp.min(jnp.where(hit, lane, float(ROUTER_LANES)), axis=-1, keepdims=True)
    is_g = lane < N_GROUPS
    gl = jnp.where(is_g, logits, NEG)
    gmax = jnp.max(gl, axis=-1, keepdims=True)
    g_sel = first(is_g & (gl == gmax))
    g_prob = 1.0 / jnp.sum(jnp.where(is_g, jnp.exp(gl - gmax), 0.0), axis=-1, keepdims=True)
    lo = N_GROUPS + EXPERTS_PER_GROUP * g_sel
    in_grp = (lane >= lo) & (lane < lo + EXPERTS_PER_GROUP)
    el = jnp.where(in_grp, logits, NEG)
    v1 = jnp.max(el, axis=-1, keepdims=True)
    i1 = first(in_grp & (el == v1))
    others = in_grp & (lane != i1)
    el2 = jnp.where(others, el, NEG)
    v2 = jnp.max(el2, axis=-1, keepdims=True)
    i2 = first(others & (el2 == v2))
    e2 = jnp.exp(v2 - v1)
    w1 = g_prob / (1.0 + e2)
    w2 = g_prob * e2 / (1.0 + e2)
    sel = jnp.where(lane == 0.0, i1 - N_GROUPS, jnp.where(lane == 1.0, i2 - N_GROUPS, 0.0))
    sel_ref[...] = jnp.where(lane == 2.0, w1, jnp.where(lane == 3.0, w2, sel))


def _router_kernel(xp_ref, xs_ref, g_ref, shp_ref, scp_ref, shs_ref, scs_ref, wr_ref, br_ref, h_ref, sel_ref,
                   *, prompt_tiles):
    i = pl.program_id(0)

    @pl.when(i < prompt_tiles)
    def _():
        _route(xp_ref[...], g_ref[...], shp_ref[...], scp_ref[...], wr_ref[...], br_ref[...], h_ref, sel_ref)

    @pl.when(i >= prompt_tiles)
    def _():
        _route(xs_ref[...], g_ref[...], shs_ref[...], scs_ref[...], wr_ref[...], br_ref[...], h_ref, sel_ref)


def _router(xp, xs, g, mod_p, mod_s, layer, wr, br):
    bp, sp, d = xp.shape
    _, bs, _ = xs.shape
    tm = math.gcd(math.gcd(sp, bs), LANES)
    tpb = sp // tm
    pt = bp * tpb
    st = bs // tm
    n_all = bp * sp + bs
    pidx = lambda i: jnp.minimum(i, pt - 1)
    sidx = lambda i: jnp.maximum(i - pt, 0)
    mp = lambda chunk: pl.BlockSpec((None, None, 1, d), lambda i: (layer, pidx(i) // tpb, 0, chunk))
    ms = lambda chunk: pl.BlockSpec((None, None, tm, d), lambda i: (layer, 0, sidx(i), chunk))
    return pl.pallas_call(
        functools.partial(_router_kernel, prompt_tiles=pt),
        grid=(pt + st,),
        in_specs=[pl.BlockSpec((None, tm, d), lambda i: (pidx(i) // tpb, pidx(i) % tpb, 0)),
                  pl.BlockSpec((None, tm, d), lambda i: (0, sidx(i), 0)),
                  pl.BlockSpec((None, 1, d), lambda i: (layer, 0, 0)),
                  mp(3), mp(4), ms(3), ms(4),
                  pl.BlockSpec((None, d, ROUTER_LANES), lambda i: (layer, 0, 0)),
                  pl.BlockSpec((None, 1, ROUTER_LANES), lambda i: (layer, 0, 0))],
        out_specs=[pl.BlockSpec((tm, d // LANES, LANES), lambda i: (i, 0, 0)),
                   pl.BlockSpec((tm, ROUTER_LANES), lambda i: (i, 0))],
        out_shape=[jax.ShapeDtypeStruct((n_all, d // LANES, LANES), F32),
                   jax.ShapeDtypeStruct((n_all, ROUTER_LANES), F32)],
        compiler_params=_cparams("parallel"),
        name="router",
    )(xp, xs, g, mod_p, mod_p, mod_s, mod_s, wr, br)


MOE_TM = 256
TOP_K = 2


def _route_slots(sel, tm):
    n = sel.shape[0]
    n_pairs = TOP_K * n
    n_tiles = (n_pairs + N_EXPERTS * (tm - 1)) // tm + 1
    p_max = n_tiles * tm
    eid = sel[:, :TOP_K].astype(jnp.int32).T.reshape(-1)
    onehot = (eid[:, None] == jnp.arange(N_EXPERTS, dtype=jnp.int32)[None, :]).astype(jnp.int32)
    cums = jnp.cumsum(onehot, axis=0)
    rank = jnp.sum(cums * onehot, axis=1) - 1
    counts = cums[-1]
    padded = ((counts + tm - 1) // tm) * tm
    ends = jnp.cumsum(padded)
    pos = (ends - padded)[eid] + rank
    pair = jnp.arange(n_pairs, dtype=jnp.int32)
    slot = jnp.arange(p_max, dtype=jnp.int32)
    dst = (n_pairs + slot % (2 * tm)).at[pos].set(pair)
    tile_start = jnp.arange(n_tiles, dtype=jnp.int32) * tm
    tile_expert = jnp.minimum(jnp.sum((ends[None, :] <= tile_start[:, None]).astype(jnp.int32), axis=1),
                              N_EXPERTS - 1)
    n_used = (ends[-1] // tm).reshape(1)
    return tile_expert, dst, n_used


DMA_ISSUE_UNROLL = 8


def _experts_kernel(te_ref, dst_ref, nu_ref, h_hbm, wg_ref, wu_ref, wd_ref, y_hbm,
                    xbuf, obuf, wgb, wub, wdb, gsem, ssem):
    t = pl.program_id(0)
    n_used = nu_ref[0]
    slot = t % 2
    tm = xbuf.shape[1]
    n = h_hbm.shape[0]
    n_pairs = y_hbm.shape[0] - 2 * tm

    def gather(tile, s):
        def body(r, c):
            pair = dst_ref[tile * tm + r]
            token = jnp.where(pair >= n, pair - n, pair)
            token = jnp.where(pair >= n_pairs, 0, token)
            pltpu.make_async_copy(h_hbm.at[pl.ds(token, 1)], xbuf.at[s, pl.ds(r, 1)], gsem.at[s]).start()
            return c
        lax.fori_loop(0, tm, body, 0, unroll=DMA_ISSUE_UNROLL)

    def scatter(tile, s):
        def body(r, c):
            pltpu.make_async_copy(obuf.at[s, pl.ds(r, 1)], y_hbm.at[pl.ds(dst_ref[tile * tm + r], 1)],
                                  ssem.at[s]).start()
            return c
        lax.fori_loop(0, tm, body, 0, unroll=DMA_ISSUE_UNROLL)

    def wait_gather(s):
        pltpu.make_async_copy(h_hbm.at[pl.ds(0, tm)], xbuf.at[s], gsem.at[s]).wait()

    def wait_scatter(s):
        pltpu.make_async_copy(obuf.at[s], y_hbm.at[pl.ds(0, tm)], ssem.at[s]).wait()

    @pl.when(t == 0)
    def _():
        obuf[0] = jnp.zeros(obuf.shape[1:], F32)
        for half in range(2):
            fill = pltpu.make_async_copy(obuf.at[0], y_hbm.at[pl.ds(n_pairs + half * tm, tm)], ssem.at[0])
            fill.start()
            fill.wait()
        gather(0, 0)

    @pl.when(t < n_used)
    def _():
        wait_gather(slot)

        @pl.when(t + 1 < n_used)
        def _():
            gather(t + 1, 1 - slot)

        @pl.when((t == 0) | (te_ref[t] != te_ref[jnp.maximum(t - 1, 0)]))
        def _():
            wgb[...] = wg_ref[...].astype(BF16)
            wub[...] = wu_ref[...].astype(BF16)
            wdb[...] = wd_ref[...].astype(BF16)

        @pl.when(t >= 2)
        def _():
            wait_scatter(slot)

        x = _from_tile_rows(xbuf.at[slot]).astype(BF16)
        a = jnp.dot(x, wgb[...], preferred_element_type=F32)
        u = jnp.dot(x, wub[...], preferred_element_type=F32)
        hid = (a * jax.nn.sigmoid(a)) * u
        _to_tile_rows(obuf.at[slot], jnp.dot(hid.astype(BF16), wdb[...], preferred_element_type=F32))
        scatter(t, slot)

        @pl.when(t == n_used - 1)
        def _():
            @pl.when(t >= 1)
            def _():
                wait_scatter(1 - slot)
            wait_scatter(slot)


def _experts(h, sel, wg, wu, wd, layer):
    n, s8, _ = h.shape
    d = s8 * LANES
    de = wg.shape[3]
    tm = MOE_TM
    tile_expert, dst, n_used = _route_slots(sel, tm)
    n_tiles = tile_expert.shape[0]
    wspec = lambda a, b: pl.BlockSpec((None, None, a, b), lambda t, te, d_, nu: (layer, te[t], 0, 0))
    grid_spec = pltpu.PrefetchScalarGridSpec(
        num_scalar_prefetch=3,
        grid=(n_tiles,),
        in_specs=[pl.BlockSpec(memory_space=pl.ANY),
                  wspec(d, de), wspec(d, de), wspec(de, d)],
        out_specs=pl.BlockSpec(memory_space=pl.ANY),
        scratch_shapes=[pltpu.VMEM((2, tm, s8, LANES), F32), pltpu.VMEM((2, tm, s8, LANES), F32),
                        pltpu.VMEM((d, de), BF16), pltpu.VMEM((d, de), BF16), pltpu.VMEM((de, d), BF16),
                        pltpu.SemaphoreType.DMA((2,)), pltpu.SemaphoreType.DMA((2,))],
    )
    return pl.pallas_call(
        _experts_kernel,
        grid_spec=grid_spec,
        out_shape=jax.ShapeDtypeStruct((TOP_K * n + 2 * tm, s8, LANES), F32),
        compiler_params=_cparams("arbitrary"),
        name="experts",
    )(tile_expert, dst, n_used, h, wg, wu, wd)


def _combine_kernel(x_ref, y0_ref, y1_ref, sel_ref, gate_ref, gf_ref, out_ref, *, final_norm):
    sel = sel_ref[...]
    moe = sel[:, TOP_K:TOP_K + 1] * _from_tile_rows(y0_ref) + sel[:, TOP_K + 1:TOP_K + 2] * _from_tile_rows(y1_ref)
    y = x_ref[...] + gate_ref[...] * moe
    if final_norm:
        y = y * lax.rsqrt(jnp.mean(y * y, axis=-1, keepdims=True) + NORM_EPS) * gf_ref[...]
    out_ref[...] = y


def _combine(x, y, sel, row0, n_all, mod, layer, g_final, final_norm):
    nb, r, d = x.shape
    tm = _largest_tile(r, 128, 8)
    assert row0 % tm == 0 and n_all % tm == 0
    xspec = pl.BlockSpec((None, tm, d), lambda b, i: (b, i, 0))
    blk = lambda k: lambda b, i: (k * n_all + row0) // tm + b * (r // tm) + i
    yspec = lambda k: pl.BlockSpec((tm,) + y.shape[1:], lambda b, i: (blk(k)(b, i), 0, 0))
    return pl.pallas_call(
        functools.partial(_combine_kernel, final_norm=final_norm),
        grid=(nb, r // tm),
        in_specs=[xspec, yspec(0), yspec(1), pl.BlockSpec((tm, ROUTER_LANES), lambda b, i: (blk(0)(b, i), 0)),
                  _mod_spec(mod, layer, 5, tm, d),
                  pl.BlockSpec((1, d), lambda b, i: (0, 0))],
        out_specs=xspec,
        out_shape=jax.ShapeDtypeStruct((nb, r, d), F32),
        compiler_params=_cparams("parallel", "parallel"),
        name="combine",
    )(x, y, y, sel, mod, g_final)


def _kv_rows(k, v, heads, hd):
    return jnp.stack([k, v], axis=-2).reshape(k.shape[:-1] + (2, heads, hd))


def kernel(x_prompt, x_sample, cache_a0_kv, cache_a1_kv, cache_a2_kv, cache_b_kv, cache_c_kv, page_table,
           c_prompt, c_sample, rel_table, w_ada, b_ada, g_mix, g_ffn, g_final, w_in_a, w_out_a, w_in_b, sinks_b,
           w_out_b, w_in_c, lambda_c, subln_c, w_out_c, w_grp, b_grp, w_rt, b_rt, w_gate, w_up, w_down):
    depth, d = g_mix.shape
    bp, sp, _ = x_prompt.shape
    bs, ts, _ = x_sample.shape
    assert ts == 1
    a_caches_t = [c.transpose(0, 1, 3, 4, 5, 2) for c in (cache_a0_kv, cache_a1_kv, cache_a2_kv)]
    cache_b_t = cache_b_kv.transpose(0, 1, 3, 4, 5, 2)
    table = rel_table.astype(F32)

    m_all = _ada_all(jnp.concatenate([c_prompt, c_sample], axis=0), w_ada, b_ada)
    mod_p = m_all[:, :bp].reshape(depth, bp, 1, 6 * d)
    mod_s = m_all[:, bp:].reshape(depth, 1, bs, 6 * d)
    g_mix3 = g_mix.reshape(depth, 1, d)
    g_ffn3 = g_ffn.reshape(depth, 1, d)
    g_fin = g_final.reshape(1, d)

    w_in = [w_in_a.astype(BF16), w_in_b.astype(BF16), w_in_c.astype(BF16)]
    w_out = [w_out_a.astype(BF16), w_out_b.astype(BF16), w_out_c.astype(BF16)]
    wr = jnp.concatenate([w_grp, w_rt.transpose(0, 2, 1, 3).reshape(depth, d, N_EXPERTS)], axis=-1)
    wr = jnp.pad(wr, ((0, 0), (0, 0), (0, ROUTER_LANES - wr.shape[-1])))
    br = jnp.concatenate([b_grp, b_rt.reshape(depth, N_EXPERTS)], axis=-1)
    br = jnp.pad(br, ((0, 0), (0, ROUTER_LANES - br.shape[-1]))).reshape(depth, 1, ROUTER_LANES)

    a_bias = [_band_bias(table, r, jnp.arange(A_HEADS)) for r in A_DILATIONS]
    wa = A_HEADS * A_HD

    xp = x_prompt
    xs = x_sample.reshape(1, bs, d)
    a_p = [[] for _ in range(A_GROUPS)]
    a_new = [None] * A_GROUPS
    b_p, b_s, c_p, c_s = [], [], [], []
    for i in range(depth):
        kind, li = i % N_MIXERS, i // N_MIXERS
        qkv_p = _norm_mod_matmul(xp, g_mix3, mod_p, i, w_in[kind][li])
        qkv_s = _norm_mod_matmul(xs, g_mix3, mod_s, i, w_in[kind][li])[0]
        if kind == 0:
            res_p = [_a_prompt_group(qkv_p, g, a_bias[g]) for g in range(A_GROUPS)]
            res_s = [_a_sample_group(qkv_s, a_caches_t[g], a_new[g], li, g, table) for g in range(A_GROUPS)]
            a_new = [r[2] for r in res_s]
            xp = _proj_res([r[0] for r in res_p], [r[1] for r in res_p], w_out[0][li], xp, mod_p, i)
            xs = _proj_res([r[0][None] for r in res_s], [r[1][None] for r in res_s], w_out[0][li], xs, mod_s, i)
            for g, L in enumerate(A_WINDOWS):
                lp = min(L, sp)
                kc, vc = (A_GROUPS + g) * wa, (2 * A_GROUPS + g) * wa
                a_p[g].append(_kv_rows(qkv_p[:, sp - lp:, kc:kc + wa], qkv_p[:, sp - lp:, vc:vc + wa], A_HEADS, A_HD))
        elif kind == 1:
            o_p = _b_prompt(qkv_p, sinks_b[li], table)
            o_s = _b_sample(qkv_s, cache_b_t, li, sinks_b[li], table)
            xp = _proj_res([o_p], [], w_out[1][li], xp, mod_p, i)
            xs = _proj_res([o_s[None]], [], w_out[1][li], xs, mod_s, i)
            nq, nk = B_HEADS * B_HD, B_KV_HEADS * B_HD
            lp = min(B_WINDOW, sp)
            b_p.append(_kv_rows(qkv_p[:, sp - lp:, nq:nq + nk], qkv_p[:, sp - lp:, nq + nk:], B_KV_HEADS, B_HD))
            b_s.append(_kv_rows(qkv_s[:, nq:nq + nk], qkv_s[:, nq + nk:], B_KV_HEADS, B_HD))
        else:
            lam_init = 0.8 - 0.6 * math.exp(-0.3 * i)
            o_p = _c_prompt(qkv_p, lambda_c[li], subln_c[li], table, lam_init)
            o_s = _c_sample(qkv_s, cache_c_kv, li, page_table, lambda_c[li], subln_c[li], table, lam_init)
            xp = _proj_res([o_p], [], w_out[2][li], xp, mod_p, i)
            xs = _proj_res([o_s[None]], [], w_out[2][li], xs, mod_s, i)
            nq, nk = C_HEADS * C_VD, C_KV_HEADS * C_VD
            c_p.append(_kv_rows(qkv_p[:, :, nq:nq + nk], qkv_p[:, :, nq + nk:], C_KV_HEADS, C_VD))
            c_s.append(_kv_rows(qkv_s[:, None, nq:nq + nk], qkv_s[:, None, nq + nk:], C_KV_HEADS, C_VD))
        last = i == depth - 1
        n_p, n_all = bp * sp, bp * sp + bs
        h_all, sel = _router(xp, xs, g_ffn3, mod_p, mod_s, i, wr, br)
        y = _experts(h_all, sel, w_gate, w_up, w_down, i)
        xp = _combine(xp, y, sel, 0, n_all, mod_p, i, g_fin, last)
        xs = _combine(xs, y, sel, n_p, n_all, mod_s, i, g_fin, last)

    def shifted(cache, new_rows):
        return jnp.concatenate([cache[:, :, 1:], jnp.stack(new_rows)[:, :, None]], axis=2)

    a_out = [c.transpose(0, 1, 5, 2, 3, 4) for c in a_new]
    return (xp, xs.reshape(bs, ts, d),
            jnp.stack(a_p[0]), a_out[0],
            jnp.stack(a_p[1]), a_out[1],
            jnp.stack(a_p[2]), a_out[2],
            jnp.stack(b_p), shifted(cache_b_kv, b_s),
            jnp.stack(c_p), jnp.stack(c_s))
```
